```python
import jax, jax.numpy as jnp
from jax import lax
import numpy as np

D_MODEL = 2048
BATCH = 4
SEQ = 8192
DEPTH = 4
DEC_BATCH = 8
DEC_SEQ = 64
PAST_LEN = 1024

CHUNK = 64
N_META = 16
N_MIXERS = 3
N_MLSTM_LAYERS = (DEPTH + 2) // 3
N_SWA_LAYERS = (DEPTH + 1) // 3
N_POOL_LAYERS = DEPTH // 3
EPS = 1e-6
F32 = jnp.float32

ML_HEADS = 8
ML_DK = D_MODEL // 16
ML_DV = D_MODEL // 8
ML_QK_W = ML_HEADS * ML_DK
ML_V_W = ML_HEADS * ML_DV
ML_IN_W = 2 * ML_QK_W + 3 * ML_V_W + 2 * ML_HEADS

SW_HEADS = D_MODEL // 64
SW_KV = SW_HEADS // 8
SW_HD = 64
SW_GROUP = SW_HEADS // SW_KV
SW_W = SW_HEADS * SW_HD
SW_KV_W = SW_KV * SW_HD
SW_IN_W = 2 * SW_W + 2 * SW_KV_W
WINDOW = 128
WIN_CHUNKS = WINDOW // CHUNK
ROT_DIM = SW_HD // 4
ROPE_THETA = 500000.0

POOL_W = D_MODEL
POOL_WINDOWS = (2, 4, 8, 16)
POOL_GROUPS = len(POOL_WINDOWS)
POOL_GC = POOL_W // POOL_GROUPS
POOL_STATE = max(POOL_WINDOWS) - 1

kernel_name = "hybrid_streaming_mlstm_swa_pool_step"


def rms_norm(x, g):
    xf = x.astype(F32)
    y = xf * lax.rsqrt(jnp.mean(xf * xf, -1, keepdims=True) + EPS)
    return (y * g.astype(F32)).astype(x.dtype)


def mlstm_chunk(q, k, v, log_i, log_f, C, n, m):
    L = q.shape[2]
    b = jnp.cumsum(log_f, -1)
    causal = jnp.tril(jnp.ones((L, L), bool))
    dmat = jnp.where(causal, b[..., :, None] - b[..., None, :] + log_i[..., None, :], -jnp.inf)
    inter = b + m[..., None]
    m_t = jnp.maximum(inter, jnp.max(dmat, -1))
    p = jnp.exp(dmat - m_t[..., None])
    w_inter = jnp.exp(inter - m_t)
    s = p * jnp.einsum('bhtd,bhsd->bhts', q, k)
    num = w_inter[..., None] * jnp.einsum('bhtd,bhvd->bhtv', q, C) + jnp.einsum('bhts,bhsv->bhtv', s, v)
    den = w_inter * jnp.einsum('bhtd,bhd->bht', q, n) + jnp.sum(s, -1)
    h = num / jnp.maximum(jnp.abs(den), jnp.exp(-m_t))[..., None]
    b_end = b[..., -1]
    a_end = b_end[..., None] - b + log_i
    m_new = jnp.maximum(b_end + m, jnp.max(a_end, -1))
    w_old = jnp.exp(b_end + m - m_new)
    w_s = jnp.exp(a_end - m_new[..., None])
    C_new = w_old[..., None, None] * C + jnp.einsum('bhs,bhsv,bhsd->bhvd', w_s, v, k)
    n_new = w_old[..., None] * n + jnp.einsum('bhs,bhsd->bhd', w_s, k)
    return h, C_new, n_new, m_new


def mlstm_project(h, w_in, b_if):
    B, L, _ = h.shape
    u = h @ w_in
    cuts = np.cumsum([ML_QK_W, ML_QK_W, ML_V_W, ML_V_W, ML_V_W, ML_HEADS]).tolist()
    q, k, v, o, z, ig, fg = jnp.split(u, cuts, axis=-1)
    def heads(t, d):
        return t.reshape(B, L, ML_HEADS, d).transpose(0, 2, 1, 3).astype(F32)
    q = heads(q, ML_DK)
    k = heads(k, ML_DK) * (ML_DK ** -0.5)
    v = heads(v, ML_DV)
    log_i = (ig.astype(F32) + b_if[0].astype(F32)).transpose(0, 2, 1)
    log_f = jax.nn.log_sigmoid(fg.astype(F32) + b_if[1].astype(F32)).transpose(0, 2, 1)
    return q, k, v, log_i, log_f, o, z


def mlstm_out(hc, o, z, norm_g, w_out, dtype):
    B, _, L, _ = hc.shape
    hn = hc * lax.rsqrt(jnp.mean(hc * hc, -1, keepdims=True) + EPS)
    hn = hn.transpose(0, 2, 1, 3).reshape(B, L, ML_V_W) * norm_g.astype(F32)
    y = hn * jax.nn.sigmoid(o.astype(F32)) * jax.nn.silu(z.astype(F32))
    return y.astype(dtype) @ w_out


def mlstm_prompt(h, w_in, b_if, norm_g, w_out):
    B, L, _ = h.shape
    q, k, v, li, lf, o, z = mlstm_project(h, w_in, b_if)
    C0 = jnp.zeros((B, ML_HEADS, ML_DV, ML_DK), F32)
    n0 = jnp.zeros((B, ML_HEADS, ML_DK), F32)
    m0 = jnp.zeros((B, ML_HEADS), F32)
    h_meta, C, n, m = mlstm_chunk(q[:, :, :N_META], k[:, :, :N_META], v[:, :, :N_META],
                                  li[..., :N_META], lf[..., :N_META], C0, n0, m0)
    nc = (L - N_META) // CHUNK
    def to_chunks(t):
        t = t[:, :, N_META:]
        return jnp.moveaxis(t.reshape(t.shape[:2] + (nc, CHUNK) + t.shape[3:]), 2, 0)
    def step(carry, xs):
        hc, C2, n2, m2 = mlstm_chunk(*xs, *carry)
        return (C2, n2, m2), hc
    (C, n, m), h_real = lax.scan(step, (C, n, m), tuple(to_chunks(t) for t in (q, k, v, li, lf)))
    h_real = jnp.moveaxis(h_real, 0, 2).reshape(B, ML_HEADS, nc * CHUNK, ML_DV)
    hc = jnp.concatenate([h_meta, h_real], 2)
    return mlstm_out(hc, o, z, norm_g, w_out, h.dtype), C, n, m


def mlstm_sample(h, C, n, m, w_in, b_if, norm_g, w_out):
    q, k, v, li, lf, o, z = mlstm_project(h, w_in, b_if)
    hc, C2, n2, m2 = mlstm_chunk(q, k, v, li, lf, C.astype(F32), n.astype(F32), m.astype(F32))
    return mlstm_out(hc, o, z, norm_g, w_out, h.dtype), C2, n2, m2


def rope_partial(x, pos):
    half = ROT_DIM // 2
    inv = jnp.power(ROPE_THETA, -jnp.arange(half, dtype=F32) / half)
    ang = pos.astype(F32)[:, None] * inv[None, :]
    cos = jnp.cos(ang)[None, :, None, :]
    sin = jnp.sin(ang)[None, :, None, :]
    xr = x[..., :ROT_DIM].astype(F32)
    x1, x2 = xr[..., :half], xr[..., half:]
    rot = jnp.concatenate([x1 * cos - x2 * sin, x2 * cos + x1 * sin], -1)
    return jnp.concatenate([rot.astype(x.dtype), x[..., ROT_DIM:]], -1)


def sink_attention(q, k, v, valid, sinks):
    s = jnp.einsum('...qkgd,...skd->...kgqs', q.astype(F32), k.astype(F32)) * (SW_HD ** -0.5)
    if valid is not None:
        s = jnp.where(valid[..., None, None, None, :], s, -jnp.inf)
    sink = sinks.astype(F32).reshape(SW_KV, SW_GROUP)[:, :, None]
    mx = jnp.maximum(jnp.max(s, -1), sink)
    p = jnp.exp(s - mx[..., None])
    den = jnp.sum(p, -1) + jnp.exp(sink - mx)
    o = jnp.einsum('...kgqs,...skd->...qkgd', p, v.astype(F32))
    return o / jnp.moveaxis(den, -1, -3)[..., None]


def swa_project(h, w_in, pos):
    B, L, _ = h.shape
    q, k, v, z = jnp.split(h @ w_in, [SW_W, SW_W + SW_KV_W, SW_W + 2 * SW_KV_W], axis=-1)
    q = rope_partial(q.reshape(B, L, SW_HEADS, SW_HD), pos).reshape(B, L, SW_KV, SW_GROUP, SW_HD)
    k = rope_partial(k.reshape(B, L, SW_KV, SW_HD), pos)
    v = v.reshape(B, L, SW_KV, SW_HD)
    return q, k, v, z


def swa_out(o, z, w_out, dtype):
    B, L = z.shape[:2]
    y = o.reshape(B, L, SW_W) * jax.nn.silu(z.astype(F32))
    return y.astype(dtype) @ w_out


def swa_prompt(h, w_in, sinks, w_out):
    B, L, _ = h.shape
    q, k, v, z = swa_project(h, w_in, jnp.arange(L))
    o_meta = sink_attention(q[:, :N_META], k[:, :N_META], v[:, :N_META], None, sinks)
    nc = (L - N_META) // CHUNK
    pad = WINDOW - N_META
    def band(t):
        tp = jnp.pad(t, ((0, 0), (pad, 0), (0, 0), (0, 0))).reshape(B, nc + WIN_CHUNKS, CHUNK, SW_KV, SW_HD)
        return jnp.concatenate([tp[:, j:j + nc] for j in range(WIN_CHUNKS + 1)], axis=2)
    rows = jnp.arange((nc + WIN_CHUNKS) * CHUNK).reshape(nc + WIN_CHUNKS, CHUNK)
    valid = jnp.concatenate([rows[j:j + nc] for j in range(WIN_CHUNKS + 1)], axis=1) >= pad
    qr = q[:, N_META:].reshape(B, nc, CHUNK, SW_KV, SW_GROUP, SW_HD)
    o_real = sink_attention(qr, band(k), band(v), valid, sinks).reshape(B, nc * CHUNK, SW_KV, SW_GROUP, SW_HD)
    o = jnp.concatenate([o_meta, o_real], 1)
    return swa_out(o, z, w_out, h.dtype), k[:, -WINDOW:], v[:, -WINDOW:]


def swa_sample(h, k_cache, v_cache, w_in, sinks, w_out):
    S = h.shape[1]
    q, k, v, z = swa_project(h, w_in, N_META + PAST_LEN + jnp.arange(S))
    kk = jnp.concatenate([k_cache.astype(k.dtype), k], 1)
    vv = jnp.concatenate([v_cache.astype(v.dtype), v], 1)
    o = sink_attention(q, kk, vv, None, sinks)
    return swa_out(o, z, w_out, h.dtype), kk[:, -WINDOW:], vv[:, -WINDOW:]


def pool_branch(h, prev, w_in, w_grp, scale, w_out, from_start):
    B, L, _ = h.shape
    u, z = jnp.split(h @ w_in, 2, axis=-1)
    ue = jnp.concatenate([prev.astype(u.dtype), u], 1)
    cs = jnp.concatenate([jnp.zeros((B, 1, POOL_W), F32), jnp.cumsum(ue.astype(F32), 1)], 1)
    hi = cs[:, POOL_STATE + 1:POOL_STATE + 1 + L]
    groups = []
    for g, w in enumerate(POOL_WINDOWS):
        sl = slice(g * POOL_GC, (g + 1) * POOL_GC)
        lo = cs[:, POOL_STATE + 1 - w:POOL_STATE + 1 - w + L, sl]
        if from_start:
            cnt = jnp.minimum(jnp.float32(w), jnp.arange(1, L + 1, dtype=F32))
        else:
            cnt = jnp.full((L,), w, F32)
        groups.append((hi[..., sl] - lo) / cnt[None, :, None])
    pooled = jnp.concatenate(groups, -1) - u.astype(F32)
    mixed = jnp.einsum('blgc,gcd->blgd', pooled.reshape(B, L, POOL_GROUPS, POOL_GC), w_grp.astype(F32))
    mixed = mixed.reshape(B, L, POOL_W) * scale.astype(F32)
    y = (mixed * jax.nn.silu(z.astype(F32))).astype(h.dtype) @ w_out
    return y, ue[:, -POOL_STATE:]


def setup_inputs(seed: int = 0) -> dict:
    key = jax.random.key(seed)
    ks = jax.random.split(key, 24)
    def nrm(k, shape, scale=1.0):
        return jax.random.normal(k, shape, F32) * scale
    b_if = jnp.stack([nrm(ks[12], (N_MLSTM_LAYERS, ML_HEADS), 0.1),
                      jnp.linspace(3.0, 6.0, ML_HEADS)[None, :] + nrm(ks[13], (N_MLSTM_LAYERS, ML_HEADS), 0.1)], axis=1)
    return {
        "x_prompt": nrm(ks[0], (BATCH, SEQ, D_MODEL)),
        "x_sample": nrm(ks[1], (DEC_BATCH, DEC_SEQ, D_MODEL)),
        "state_mlstm_C": nrm(ks[2], (N_MLSTM_LAYERS, DEC_BATCH, ML_HEADS, ML_DV, ML_DK), 0.3),
        "state_mlstm_n": nrm(ks[3], (N_MLSTM_LAYERS, DEC_BATCH, ML_HEADS, ML_DK), 0.3),
        "state_mlstm_m": jax.random.uniform(ks[4], (N_MLSTM_LAYERS, DEC_BATCH, ML_HEADS), F32, 0.0, 2.0),
        "cache_swa_k": nrm(ks[5], (N_SWA_LAYERS, DEC_BATCH, WINDOW, SW_KV, SW_HD)),
        "cache_swa_v": nrm(ks[6], (N_SWA_LAYERS, DEC_BATCH, WINDOW, SW_KV, SW_HD)),
        "state_pool": nrm(ks[7], (N_POOL_LAYERS, DEC_BATCH, POOL_STATE, POOL_W)),
        "meta_tokens": nrm(ks[8], (N_META, D_MODEL)),
        "norm_g": 1.0 + nrm(ks[9], (DEPTH, D_MODEL), 0.01),
        "final_norm_g": 1.0 + nrm(ks[10], (D_MODEL,), 0.01),
        "mlstm_w_in": nrm(ks[11], (N_MLSTM_LAYERS, D_MODEL, ML_IN_W), D_MODEL ** -0.5),
        "mlstm_b_if": b_if,
        "mlstm_norm_g": 1.0 + nrm(ks[14], (N_MLSTM_LAYERS, ML_V_W), 0.01),
        "mlstm_w_out": nrm(ks[15], (N_MLSTM_LAYERS, ML_V_W, D_MODEL), ML_V_W ** -0.5),
        "swa_w_in": nrm(ks[16], (N_SWA_LAYERS, D_MODEL, SW_IN_W), D_MODEL ** -0.5),
        "swa_sinks": nrm(ks[17], (N_SWA_LAYERS, SW_HEADS), 0.5),
        "swa_w_out": nrm(ks[18], (N_SWA_LAYERS, SW_W, D_MODEL), SW_W ** -0.5),
        "pool_w_in": nrm(ks[19], (N_POOL_LAYERS, D_MODEL, 2 * POOL_W), D_MODEL ** -0.5),
        "pool_w_grp": nrm(ks[20], (N_POOL_LAYERS, POOL_GROUPS, POOL_GC, POOL_GC), POOL_GC ** -0.5),
        "pool_scale": 1.0 + nrm(ks[21], (N_POOL_LAYERS, POOL_W), 0.1),
        "pool_w_out": nrm(ks[22], (N_POOL_LAYERS, POOL_W, D_MODEL), POOL_W ** -0.5),
    }


def reference(x_prompt, x_sample, state_mlstm_C, state_mlstm_n, state_mlstm_m, cache_swa_k, cache_swa_v,
              state_pool, meta_tokens, norm_g, final_norm_g, mlstm_w_in, mlstm_b_if, mlstm_norm_g, mlstm_w_out,
              swa_w_in, swa_sinks, swa_w_out, pool_w_in, pool_w_grp, pool_scale, pool_w_out):
    B = x_prompt.shape[0]
    meta = jnp.broadcast_to(meta_tokens.astype(x_prompt.dtype)[None], (B, N_META, D_MODEL))
    xp = jnp.concatenate([meta, x_prompt], 1)
    xs = x_sample
    pC, pn, pm, pk, pv, pp = [], [], [], [], [], []
    sC, sn, sm, sk, sv, sp = [], [], [], [], [], []
    for i in range(DEPTH):
        kind, j = i % N_MIXERS, i // N_MIXERS
        hp = rms_norm(xp, norm_g[i])
        hs = rms_norm(xs, norm_g[i])
        if kind == 0:
            yp, C1, n1, m1 = mlstm_prompt(hp, mlstm_w_in[j], mlstm_b_if[j], mlstm_norm_g[j], mlstm_w_out[j])
            ys, C2, n2, m2 = mlstm_sample(hs, state_mlstm_C[j], state_mlstm_n[j], state_mlstm_m[j],
                                          mlstm_w_in[j], mlstm_b_if[j], mlstm_norm_g[j], mlstm_w_out[j])
            pC.append(C1); pn.append(n1); pm.append(m1)
            sC.append(C2); sn.append(n2); sm.append(m2)
        elif kind == 1:
            yp, k1, v1 = swa_prompt(hp, swa_w_in[j], swa_sinks[j], swa_w_out[j])
            ys, k2, v2 = swa_sample(hs, cache_swa_k[j], cache_swa_v[j], swa_w_in[j], swa_sinks[j], swa_w_out[j])
            pk.append(k1); pv.append(v1); sk.append(k2); sv.append(v2)
        else:
            zeros_prev = jnp.zeros((B, POOL_STATE, POOL_W), hp.dtype)
            yp, u1 = pool_branch(hp, zeros_prev, pool_w_in[j], pool_w_grp[j], pool_scale[j], pool_w_out[j], True)
            ys, u2 = pool_branch(hs, state_pool[j], pool_w_in[j], pool_w_grp[j], pool_scale[j], pool_w_out[j], False)
            pp.append(u1); sp.append(u2)
        xp = xp + yp
        xs = xs + ys
    y_prompt = rms_norm(xp[:, N_META:], final_norm_g)
    y_sample = rms_norm(xs, final_norm_g)
    return (y_prompt, y_sample,
            jnp.stack(pC), jnp.stack(pn), jnp.stack(pm), jnp.stack(pk), jnp.stack(pv), jnp.stack(pp),
            jnp.stack(sC), jnp.stack(sn), jnp.stack(sm), jnp.stack(sk), jnp.stack(sv), jnp.stack(sp))
```

```python
import functools

import jax
import jax.numpy as jnp
from jax import lax
from jax.experimental import pallas as pl
from jax.experimental.pallas import tpu as pltpu

F32 = jnp.float32
MXU_DTYPE = jnp.bfloat16
ACT_DTYPE = jnp.bfloat16

EPS = 1e-6
N_META = 16
N_MIXERS = 3
PAST_LEN = 1024

ML_HEADS = 8
ML_DK = 128
ML_DV = 256
ML_QK_W = ML_HEADS * ML_DK
ML_V_W = ML_HEADS * ML_DV
ML_MAIN_W = 2 * ML_QK_W + 3 * ML_V_W
ML_GATE_PAD = 128

SW_HEADS = 32
SW_KV = 4
SW_HD = 64
SW_W = SW_HEADS * SW_HD
SW_KV_W = SW_KV * SW_HD
WINDOW = 128
SW_CHUNK = 64
ROT_DIM = 16
ROPE_THETA = 500000.0

POOL_WINDOWS = (2, 4, 8, 16)
POOL_GC = 512
POOL_HIST = 16

LANE = 128
VMEM_LIMIT_BYTES = 56 * 1024 * 1024


def _cparams(sem):
    return pltpu.CompilerParams(dimension_semantics=sem, vmem_limit_bytes=VMEM_LIMIT_BYTES)


def _nt_dot(a, b):
    return lax.dot_general(a, b, (((1,), (1,)), ((), ())), preferred_element_type=F32)


def _tn_dot(a, b):
    return lax.dot_general(a, b, (((0,), (0,)), ((), ())), preferred_element_type=F32)


def _sigmoid(x):
    return 1.0 / (1.0 + jnp.exp(-x))


def _silu(x):
    return x * _sigmoid(x)


def _log_sigmoid(x):
    return jnp.minimum(x, 0.0) - jnp.log1p(jnp.exp(-jnp.abs(x)))


def _proj_in_kernel(*refs, rope, gates):
    x_ref, g_ref, w_ref = refs[:3]
    pos = 3
    if rope:
        ta_ref, tb_ref, tc_ref, rm_ref = refs[pos:pos + 4]
        pos += 4
    if gates:
        wg_ref, wgt_ref = refs[pos:pos + 2]
        pos += 2
    o_ref = refs[pos]
    pos += 1
    if gates:
        gc_ref, gt_ref = refs[pos:pos + 2]
        pos += 2
    hn_sc = refs[pos]
    j = pl.program_id(1)

    @pl.when(j == 0)
    def _():
        x = x_ref[...]
        ms = jnp.mean(x * x, axis=1, keepdims=True)
        hn = (x * lax.rsqrt(ms + EPS) * g_ref[...]).astype(MXU_DTYPE)
        hn_sc[...] = hn
        if gates:
            gcol = jnp.dot(hn, wg_ref[...], preferred_element_type=F32)
            gc_ref[...] = gcol[:, :2 * ML_HEADS]
            gt_ref[...] = _nt_dot(wgt_ref[...], hn)

    acc = jnp.dot(hn_sc[...], w_ref[...], preferred_element_type=F32)
    if rope:
        tn = acc.shape[1]
        reps = tn // LANE
        ta = jnp.concatenate([ta_ref[...]] * reps, axis=1)
        tb = jnp.concatenate([tb_ref[...]] * reps, axis=1)
        tc = jnp.concatenate([tc_ref[...]] * reps, axis=1)
        up = pltpu.roll(acc, tn - ROT_DIM // 2, 1)
        dn = pltpu.roll(acc, ROT_DIM // 2, 1)
        rot = acc * ta + up * tb + dn * tc
        acc = jnp.where(rm_ref[...] > 0.0, rot, acc)
    o_ref[...] = acc.astype(o_ref.dtype)


def _proj_in(x, g, w, *, tm, tn, rope_tabs=None, rope_mask=None, gate_w=None):
    m, d = x.shape
    n = w.shape[1]
    rope = rope_tabs is not None
    gates = gate_w is not None
    grid = (m // tm, n // tn)
    in_specs = [pl.BlockSpec((tm, d), lambda i, j: (i, 0)),
                pl.BlockSpec((1, d), lambda i, j: (0, 0)),
                pl.BlockSpec((d, tn), lambda i, j: (0, j))]
    args = [x, g.reshape(1, d), w]
    if rope:
        nrep = rope_tabs[0].shape[0] // tm
        for t in rope_tabs:
            in_specs.append(pl.BlockSpec((tm, LANE), lambda i, j, nrep=nrep: (i % nrep, 0)))
            args.append(t)
        in_specs.append(pl.BlockSpec((1, tn), lambda i, j: (0, j)))
        args.append(rope_mask)
    if gates:
        wg, wgt = gate_w
        in_specs.append(pl.BlockSpec((d, ML_GATE_PAD), lambda i, j: (0, 0)))
        in_specs.append(pl.BlockSpec((2 * ML_HEADS, d), lambda i, j: (0, 0)))
        args += [wg, wgt]
    out_shape = [jax.ShapeDtypeStruct((m, n), ACT_DTYPE)]
    out_specs = [pl.BlockSpec((tm, tn), lambda i, j: (i, j))]
    if gates:
        out_shape += [jax.ShapeDtypeStruct((m, 2 * ML_HEADS), F32),
                      jax.ShapeDtypeStruct((2 * ML_HEADS, m), F32)]
        out_specs += [pl.BlockSpec((tm, 2 * ML_HEADS), lambda i, j: (i, 0)),
                      pl.BlockSpec((2 * ML_HEADS, tm), lambda i, j: (0, i))]
    res = pl.pallas_call(
        functools.partial(_proj_in_kernel, rope=rope, gates=gates),
        out_shape=out_shape, grid=grid, in_specs=in_specs, out_specs=out_specs,
        scratch_shapes=[pltpu.VMEM((tm, d), MXU_DTYPE)],
        compiler_params=_cparams(("parallel", "arbitrary")),
        name="proj_in",
    )(*args)
    return res if gates else res[0]


def _proj_out_kernel(*refs, final_norm):
    y_ref, w_ref, x_ref = refs[:3]
    if final_norm:
        g_ref, o_ref = refs[3:5]
    else:
        o_ref = refs[3]
    acc = x_ref[...] + jnp.dot(y_ref[...].astype(MXU_DTYPE), w_ref[...], preferred_element_type=F32)
    if final_norm:
        ms = jnp.mean(acc * acc, axis=1, keepdims=True)
        acc = acc * lax.rsqrt(ms + EPS) * g_ref[...]
    o_ref[...] = acc


def _proj_out(x, y, w, *, tm, final_g=None):
    m, d = x.shape
    k = y.shape[1]
    final_norm = final_g is not None
    in_specs = [pl.BlockSpec((tm, k), lambda i: (i, 0)),
                pl.BlockSpec((k, d), lambda i: (0, 0)),
                pl.BlockSpec((tm, d), lambda i: (i, 0))]
    args = [y, w, x]
    if final_norm:
        in_specs.append(pl.BlockSpec((1, d), lambda i: (0, 0)))
        args.append(final_g.reshape(1, d))
    return pl.pallas_call(
        functools.partial(_proj_out_kernel, final_norm=final_norm),
        out_shape=jax.ShapeDtypeStruct((m, d), F32), grid=(m // tm,),
        in_specs=in_specs, out_specs=pl.BlockSpec((tm, d), lambda i: (i, 0)),
        compiler_params=_cparams(("parallel",)),
        name="proj_out",
    )(*args)


def _mlstm_kernel(q_ref, k_ref, v_ref, o_ref, z_ref, gc_ref, gt_ref, bc_ref, br_ref, ng_ref,
                  c0_ref, n0_ref, m0_ref, y_ref, cs_ref, ns_ref, ms_ref, *, L):
    c = pl.program_id(1)

    @pl.when(c == 0)
    def _():
        cs_ref[...] = c0_ref[...]
        ns_ref[...] = n0_ref[...]
        ms_ref[...] = m0_ref[...]

    row = lax.broadcasted_iota(jnp.int32, (L, L), 0)
    col = lax.broadcasted_iota(jnp.int32, (L, L), 1)
    tril = col <= row
    scale = ML_DK ** -0.5

    gc = gc_ref[0] + bc_ref[...]
    lane16 = lax.broadcasted_iota(jnp.int32, gc.shape, 1)
    gc = jnp.where(lane16 < ML_HEADS, gc, _log_sigmoid(gc))
    gt = gt_ref[0, 0] + br_ref[...]
    sub16 = lax.broadcasted_iota(jnp.int32, gt.shape, 0)
    gt = jnp.where(sub16 < ML_HEADS, gt, _log_sigmoid(gt))

    for h in range(ML_HEADS):
        li_col = gc[:, h:h + 1]
        lf_col = gc[:, ML_HEADS + h:ML_HEADS + h + 1]
        li_row = gt[h:h + 1, :]
        lf_row = gt[ML_HEADS + h:ML_HEADS + h + 1, :]
        q = q_ref[0, :, h * ML_DK:(h + 1) * ML_DK]
        k = k_ref[0, :, h * ML_DK:(h + 1) * ML_DK]
        v = v_ref[0, :, h * ML_DV:(h + 1) * ML_DV]
        c_old = cs_ref[0, h]
        n_old = ns_ref[0, h:h + 1, :]
        m_old = ms_ref[0, h:h + 1, 0:1]

        b_col = jnp.sum(jnp.where(tril, lf_row, 0.0), axis=1, keepdims=True)
        b_row = jnp.sum(jnp.where(row <= col, lf_col, 0.0), axis=0, keepdims=True)
        dmat = jnp.where(tril, b_col - b_row + li_row, -jnp.inf)
        inter = b_col + m_old
        m_t = jnp.maximum(inter, jnp.max(dmat, axis=1, keepdims=True))
        p = jnp.exp(dmat - m_t)
        w_inter = jnp.exp(inter - m_t)
        s = p * (_nt_dot(q, k) * scale)
        qf = q.astype(F32)
        kf = k.astype(F32)
        num = w_inter * _nt_dot(q, c_old.astype(MXU_DTYPE)) + jnp.dot(
            s.astype(MXU_DTYPE), v, preferred_element_type=F32)
        den = w_inter * jnp.sum(qf * n_old, axis=1, keepdims=True) + jnp.sum(s, axis=1, keepdims=True)
        hh = num / jnp.maximum(jnp.abs(den), jnp.exp(-m_t))

        hn = hh * lax.rsqrt(jnp.mean(hh * hh, axis=1, keepdims=True) + EPS)
        hn = hn * ng_ref[:, h * ML_DV:(h + 1) * ML_DV]
        og = o_ref[0, :, h * ML_DV:(h + 1) * ML_DV].astype(F32)
        zg = z_ref[0, :, h * ML_DV:(h + 1) * ML_DV].astype(F32)
        y_ref[0, :, h * ML_DV:(h + 1) * ML_DV] = (hn * _sigmoid(og) * _silu(zg)).astype(y_ref.dtype)

        b_end = jnp.sum(lf_row, axis=1, keepdims=True)
        a_end = b_end - b_col + li_col
        m_new = jnp.maximum(b_end + m_old, jnp.max(a_end, axis=0, keepdims=True))
        w_old = jnp.exp(b_end + m_old - m_new)
        w_s = jnp.exp(a_end - m_new)
        vw = (v.astype(F32) * w_s).astype(MXU_DTYPE)
        cs_ref[0, h] = w_old * c_old + _tn_dot(vw, k) * scale
        ns_ref[0, h:h + 1, :] = w_old * n_old + jnp.sum(kf * w_s, axis=0, keepdims=True) * scale
        ms_ref[0, h:h + 1, :] = jnp.broadcast_to(m_new, (1, LANE))


def _mlstm_mixer(u, gc, gt, b_if, norm_g, c0, n0, m0, *, L):
    bsz, s, _ = u.shape
    nc = s // L
    shared = c0.shape[0] == 1
    st = (lambda b, c: (0, 0, 0, 0)) if shared else (lambda b, c: (b, 0, 0, 0))
    st3 = (lambda b, c: (0, 0, 0)) if shared else (lambda b, c: (b, 0, 0))
    qk_b = ML_QK_W
    v_b = ML_V_W
    in_specs = [
        pl.BlockSpec((1, L, qk_b), lambda b, c: (b, c, 0)),
        pl.BlockSpec((1, L, qk_b), lambda b, c: (b, c, 1)),
        pl.BlockSpec((1, L, v_b), lambda b, c: (b, c, 1)),
        pl.BlockSpec((1, L, v_b), lambda b, c: (b, c, 2)),
        pl.BlockSpec((1, L, v_b), lambda b, c: (b, c, 3)),
        pl.BlockSpec((1, L, 2 * ML_HEADS), lambda b, c: (b, c, 0)),
        pl.BlockSpec((1, 1, 2 * ML_HEADS, L), lambda b, c: (b, c, 0, 0)),
        pl.BlockSpec((1, 2 * ML_HEADS), lambda b, c: (0, 0)),
        pl.BlockSpec((2 * ML_HEADS, 1), lambda b, c: (0, 0)),
        pl.BlockSpec((1, ML_V_W), lambda b, c: (0, 0)),
        pl.BlockSpec((1, ML_HEADS, ML_DV, ML_DK), st),
        pl.BlockSpec((1, ML_HEADS, ML_DK), st3),
        pl.BlockSpec((1, ML_HEADS, LANE), st3),
    ]
    out_shape = [jax.ShapeDtypeStruct((bsz, s, ML_V_W), ACT_DTYPE),
                 jax.ShapeDtypeStruct((bsz, ML_HEADS, ML_DV, ML_DK), F32),
                 jax.ShapeDtypeStruct((bsz, ML_HEADS, ML_DK), F32),
                 jax.ShapeDtypeStruct((bsz, ML_HEADS, LANE), F32)]
    out_specs = [pl.BlockSpec((1, L, ML_V_W), lambda b, c: (b, c, 0)),
                 pl.BlockSpec((1, ML_HEADS, ML_DV, ML_DK), lambda b, c: (b, 0, 0, 0)),
                 pl.BlockSpec((1, ML_HEADS, ML_DK), lambda b, c: (b, 0, 0)),
                 pl.BlockSpec((1, ML_HEADS, LANE), lambda b, c: (b, 0, 0))]
    b_flat = b_if.reshape(2 * ML_HEADS).astype(F32)
    y, cN, nN, mN = pl.pallas_call(
        functools.partial(_mlstm_kernel, L=L),
        out_shape=out_shape, grid=(bsz, nc), in_specs=in_specs, out_specs=out_specs,
        compiler_params=_cparams(("parallel", "arbitrary")),
        name="mlstm_mixer",
    )(u, u, u, u, u, gc, gt, b_flat.reshape(1, -1), b_flat.reshape(-1, 1),
      norm_g.reshape(1, ML_V_W).astype(F32), c0, n0, m0)
    return y, cN, nN, mN[:, :, 0]


def _swa_kernel(*refs, Lc, nkb, n_invalid):
    sink_ref, q_ref, z_ref = refs[:3]
    kv_refs = refs[3:3 + nkb]
    y_ref = refs[3 + nkb]
    c = pl.program_id(1)
    nk = nkb * Lc
    pairs = SW_HEADS // SW_KV // 2
    rows = pairs * Lc

    kv = jnp.concatenate([r[0] for r in kv_refs], axis=0).astype(F32)
    lane = lax.broadcasted_iota(jnp.int32, (nk, LANE), 1)
    low = lane < SW_HD
    key_idx = c * Lc + lax.broadcasted_iota(jnp.int32, (1, nk), 1)
    valid = key_idx >= n_invalid
    blk = lax.broadcasted_iota(jnp.int32, (rows, 1), 0) // Lc

    def split(t, g):
        swapped = pltpu.roll(t, SW_HD, 1)
        if g % 2 == 0:
            lo, hi = t, swapped
        else:
            lo, hi = swapped, t
        return (jnp.where(low, lo, 0.0).astype(MXU_DTYPE), jnp.where(low, 0.0, hi).astype(MXU_DTYPE))

    for g in range(SW_KV):
        kt = kv[:, (g // 2) * LANE:(g // 2 + 1) * LANE]
        vt = kv[:, SW_KV_W + (g // 2) * LANE:SW_KV_W + (g // 2 + 1) * LANE]
        k_lo, k_hi = split(kt, g)
        v_lo, v_hi = split(vt, g)
        qs = jnp.concatenate([q_ref[0, :, (pairs * g + j) * LANE:(pairs * g + j + 1) * LANE]
                              for j in range(pairs)], axis=0)
        out = jnp.zeros((rows, LANE), F32)
        for par, (kk, vv) in enumerate(((k_lo, v_lo), (k_hi, v_hi))):
            sink = jnp.zeros((rows, 1), F32)
            for j in range(pairs):
                sink = jnp.where(blk == j, sink_ref[SW_HEADS // SW_KV * g + 2 * j + par], sink)
            s = _nt_dot(qs, kk) * (SW_HD ** -0.5)
            if n_invalid > 0:
                s = jnp.where(valid, s, -jnp.inf)
            mx = jnp.maximum(jnp.max(s, axis=1, keepdims=True), sink)
            p = jnp.exp(s - mx)
            den = jnp.sum(p, axis=1, keepdims=True) + jnp.exp(sink - mx)
            out = out + jnp.dot(p.astype(MXU_DTYPE), vv, preferred_element_type=F32) / den
        for j in range(pairs):
            cs = slice((pairs * g + j) * LANE, (pairs * g + j + 1) * LANE)
            zg = z_ref[0, :, cs].astype(F32)
            y_ref[0, :, cs] = (out[j * Lc:(j + 1) * Lc] * _silu(zg)).astype(y_ref.dtype)


def _swa_mixer(u, kv_ext, sinks, *, Lc, nkb, n_invalid):
    bsz, s, _ = u.shape
    nc = s // Lc
    in_specs = [pl.BlockSpec(memory_space=pltpu.SMEM),
                pl.BlockSpec((1, Lc, SW_W), lambda b, c: (b, c, 0)),
                pl.BlockSpec((1, Lc, SW_W), lambda b, c: (b, c, 1))]
    for i in range(nkb):
        in_specs.append(pl.BlockSpec((1, Lc, 2 * SW_KV_W), lambda b, c, i=i: (b, c + i, 0)))
    return pl.pallas_call(
        functools.partial(_swa_kernel, Lc=Lc, nkb=nkb, n_invalid=n_invalid),
        out_shape=jax.ShapeDtypeStruct((bsz, s, SW_W), ACT_DTYPE), grid=(bsz, nc),
        in_specs=in_specs, out_specs=pl.BlockSpec((1, Lc, SW_W), lambda b, c: (b, c, 0)),
        compiler_params=_cparams(("parallel", "parallel")),
        name="swa_mixer",
    )(sinks.astype(F32), u, u, *([kv_ext] * nkb))


def _pool_kernel(u_ref, z_ref, h0_ref, wg_ref, sc_ref, y_ref, e_sc, *, T, from_start):
    t = pl.program_id(1)

    @pl.when(t == 0)
    def _():
        e_sc[0:POOL_HIST, :] = h0_ref[0]

    u = u_ref[0].astype(F32)
    e_sc[POOL_HIST:POOL_HIST + T, :] = u
    for g, w in enumerate(POOL_WINDOWS):
        cs = slice(g * POOL_GC, (g + 1) * POOL_GC)
        ug = u[:, cs]
        acc = ug
        for d in range(1, w):
            acc = acc + e_sc[POOL_HIST - d:POOL_HIST - d + T, cs]
        if from_start:
            pos = t * T + lax.broadcasted_iota(jnp.int32, (T, 1), 0) + 1
            pooled = acc / jnp.minimum(pos, w).astype(F32)
        else:
            pooled = acc * (1.0 / w)
        pooled = pooled - ug
        mixed = jnp.dot(pooled.astype(MXU_DTYPE), wg_ref[g], preferred_element_type=F32)
        zg = z_ref[0, :, cs].astype(F32)
        y_ref[0, :, cs] = (mixed * sc_ref[:, cs] * _silu(zg)).astype(y_ref.dtype)
    e_sc[0:POOL_HIST, :] = e_sc[T:T + POOL_HIST, :]


def _pool_mixer(u, hist, w_grp, scale, *, T, from_start):
    bsz, s, w2 = u.shape
    w = w2 // 2
    nt = s // T
    hmap = (lambda b, t: (0, 0, 0)) if hist.shape[0] == 1 else (lambda b, t: (b, 0, 0))
    return pl.pallas_call(
        functools.partial(_pool_kernel, T=T, from_start=from_start),
        out_shape=jax.ShapeDtypeStruct((bsz, s, w), ACT_DTYPE), grid=(bsz, nt),
        in_specs=[pl.BlockSpec((1, T, w), lambda b, t: (b, t, 0)),
                  pl.BlockSpec((1, T, w), lambda b, t: (b, t, 1)),
                  pl.BlockSpec((1, POOL_HIST, w), hmap),
                  pl.BlockSpec((len(POOL_WINDOWS), POOL_GC, POOL_GC), lambda b, t: (0, 0, 0)),
                  pl.BlockSpec((1, w), lambda b, t: (0, 0))],
        out_specs=pl.BlockSpec((1, T, w), lambda b, t: (b, t, 0)),
        scratch_shapes=[pltpu.VMEM((POOL_HIST + T, w), F32)],
        compiler_params=_cparams(("parallel", "arbitrary")),
        name="pool_mixer",
    )(u, u, hist, w_grp, scale.reshape(1, w).astype(F32))


def _rope_tables(pos):
    half = ROT_DIM // 2
    inv = jnp.power(ROPE_THETA, -jnp.arange(half, dtype=F32) / half)
    ang = pos.astype(F32)[:, None] * inv[None, :]
    cos, sin = jnp.cos(ang), jnp.sin(ang)
    n = pos.shape[0]
    pad = jnp.zeros((n, SW_HD - ROT_DIM), F32)
    ta = jnp.concatenate([cos, cos, pad + 1.0], axis=1)
    tb = jnp.concatenate([-sin, jnp.zeros((n, half), F32), pad], axis=1)
    tc = jnp.concatenate([jnp.zeros((n, half), F32), sin, pad], axis=1)
    return tuple(jnp.concatenate([t, t], axis=1) for t in (ta, tb, tc))


def _pick_tile(m, pref):
    t = pref
    while m % t:
        t //= 2
    return t


def kernel(x_prompt, x_sample, state_mlstm_C, state_mlstm_n, state_mlstm_m, cache_swa_k, cache_swa_v, state_pool, meta_tokens, norm_g, final_norm_g, mlstm_w_in, mlstm_b_if, mlstm_norm_g, mlstm_w_out, swa_w_in, swa_sinks, swa_w_out, pool_w_in, pool_w_grp, pool_scale, pool_w_out):
    bp, sp, d = x_prompt.shape
    bs, ss, _ = x_sample.shape
    depth = norm_g.shape[0]
    mp = bp * sp
    ms_rows = bs * ss
    small = ms_rows + N_META

    xp = x_prompt.reshape(mp, d)
    xs = jnp.concatenate([x_sample.reshape(ms_rows, d), meta_tokens.astype(x_prompt.dtype)], axis=0)
    tm_p = _pick_tile(sp, 1024)
    tm_o = _pick_tile(sp, 512)
    ml_chunk = _pick_tile(sp, 256)
    pool_tile = _pick_tile(sp, 256)

    pC, pn, pm, pk, pv, pp = [], [], [], [], [], []
    sC, sn, sm, sk, sv, s_pool = [], [], [], [], [], []
    for i in range(depth):
        kind, j = i % N_MIXERS, i // N_MIXERS
        last = i == depth - 1
        fin = final_norm_g if last else None
        if kind == 0:
            w_in = mlstm_w_in[j]
            w_main = w_in[:, :ML_MAIN_W].astype(MXU_DTYPE)
            wg = w_in[:, ML_MAIN_W:]
            wg_pad = jnp.pad(wg, ((0, 0), (0, ML_GATE_PAD - 2 * ML_HEADS))).astype(MXU_DTYPE)
            gate_w = (wg_pad, wg.T.astype(MXU_DTYPE))
            up, gcp, gtp = _proj_in(xp, norm_g[i], w_main, tm=tm_p, tn=1024, gate_w=gate_w)
            us, gcs, gts = _proj_in(xs, norm_g[i], w_main, tm=small, tn=1024, gate_w=gate_w)
            zc = jnp.zeros((1, ML_HEADS, ML_DV, ML_DK), F32)
            zn = jnp.zeros((1, ML_HEADS, ML_DK), F32)
            zm = jnp.zeros((1, ML_HEADS, LANE), F32)
            ym, c_m, n_m, m_m = _mlstm_mixer(
                us[ms_rows:].reshape(1, N_META, -1), gcs[ms_rows:].reshape(1, N_META, -1),
                gts[:, ms_rows:].reshape(1, 1, 2 * ML_HEADS, N_META),
                mlstm_b_if[j], mlstm_norm_g[j], zc, zn, zm, L=N_META)
            ysm, c_s, n_s, m_s = _mlstm_mixer(
                us[:ms_rows].reshape(bs, ss, -1), gcs[:ms_rows].reshape(bs, ss, -1),
                gts[:, :ms_rows].reshape(2 * ML_HEADS, bs, 1, ss).transpose(1, 2, 0, 3),
                mlstm_b_if[j], mlstm_norm_g[j], state_mlstm_C[j].astype(F32), state_mlstm_n[j].astype(F32),
                jnp.broadcast_to(state_mlstm_m[j].astype(F32)[..., None], (bs, ML_HEADS, LANE)), L=ss)
            yp, c_p, n_p, m_p = _mlstm_mixer(
                up.reshape(bp, sp, -1), gcp.reshape(bp, sp, -1),
                gtp.reshape(2 * ML_HEADS, bp, sp // ml_chunk, ml_chunk).transpose(1, 2, 0, 3),
                mlstm_b_if[j], mlstm_norm_g[j], c_m, n_m,
                jnp.broadcast_to(m_m[..., None], (1, ML_HEADS, LANE)), L=ml_chunk)
            pC.append(c_p); pn.append(n_p); pm.append(m_p)
            sC.append(c_s); sn.append(n_s); sm.append(m_s)
            w_out = mlstm_w_out[j]
        elif kind == 1:
            w_in = swa_w_in[j]
            wq, wk, wv, wz = jnp.split(w_in, [SW_W, SW_W + SW_KV_W, SW_W + 2 * SW_KV_W], axis=1)
            w_perm = jnp.concatenate([wq, wz, wk, wv], axis=1).astype(MXU_DTYPE)
            n_in = w_perm.shape[1]
            rope_mask = jnp.concatenate([jnp.ones((1, SW_W), F32), jnp.zeros((1, SW_W), F32),
                                         jnp.ones((1, SW_KV_W), F32), jnp.zeros((1, SW_KV_W), F32)], axis=1)
            tabs_p = _rope_tables(N_META + jnp.arange(sp))
            pos_s = jnp.concatenate([jnp.tile(N_META + PAST_LEN + jnp.arange(ss), bs), jnp.arange(N_META)])
            tabs_s = _rope_tables(pos_s)
            up = _proj_in(xp, norm_g[i], w_perm, tm=tm_p, tn=512, rope_tabs=tabs_p, rope_mask=rope_mask)
            us = _proj_in(xs, norm_g[i], w_perm, tm=small, tn=512, rope_tabs=tabs_s, rope_mask=rope_mask)
            up = up.reshape(bp, sp, n_in)
            kv_p = up[:, :, 2 * SW_W:]
            kv_s = us[:ms_rows, 2 * SW_W:].reshape(bs, ss, 2 * SW_KV_W)
            kv_m = us[ms_rows:, 2 * SW_W:].reshape(1, N_META, 2 * SW_KV_W)
            ym = _swa_mixer(us[ms_rows:].reshape(1, N_META, n_in), kv_m, swa_sinks[j],
                            Lc=N_META, nkb=1, n_invalid=0)
            cache = jnp.concatenate([cache_swa_k[j].reshape(bs, WINDOW, SW_KV_W),
                                     cache_swa_v[j].reshape(bs, WINDOW, SW_KV_W)], axis=-1).astype(ACT_DTYPE)
            ysm = _swa_mixer(us[:ms_rows].reshape(bs, ss, n_in), jnp.concatenate([cache, kv_s], axis=1),
                             swa_sinks[j], Lc=SW_CHUNK, nkb=1 + WINDOW // SW_CHUNK, n_invalid=0)
            hist = jnp.concatenate([jnp.zeros((1, WINDOW - N_META, 2 * SW_KV_W), ACT_DTYPE), kv_m], axis=1)
            kv_ext = jnp.concatenate([jnp.broadcast_to(hist, (bp, WINDOW, 2 * SW_KV_W)), kv_p], axis=1)
            yp = _swa_mixer(up, kv_ext, swa_sinks[j], Lc=SW_CHUNK, nkb=1 + WINDOW // SW_CHUNK,
                            n_invalid=WINDOW - N_META)
            pk.append(kv_p[:, -WINDOW:, :SW_KV_W].astype(F32).reshape(bp, WINDOW, SW_KV, SW_HD))
            pv.append(kv_p[:, -WINDOW:, SW_KV_W:].astype(F32).reshape(bp, WINDOW, SW_KV, SW_HD))
            k_new = kv_s[:, :, :SW_KV_W].astype(F32).reshape(bs, ss, SW_KV, SW_HD)
            v_new = kv_s[:, :, SW_KV_W:].astype(F32).reshape(bs, ss, SW_KV, SW_HD)
            sk.append(jnp.concatenate([cache_swa_k[j].astype(F32), k_new], axis=1)[:, -WINDOW:])
            sv.append(jnp.concatenate([cache_swa_v[j].astype(F32), v_new], axis=1)[:, -WINDOW:])
            w_out = swa_w_out[j]
        else:
            w_in = pool_w_in[j].astype(MXU_DTYPE)
            pw = w_in.shape[1] // 2
            w_grp = pool_w_grp[j].astype(MXU_DTYPE)
            up = _proj_in(xp, norm_g[i], w_in, tm=tm_p, tn=1024).reshape(bp, sp, 2 * pw)
            us = _proj_in(xs, norm_g[i], w_in, tm=small, tn=1024)
            u_m = us[ms_rows:].reshape(1, N_META, 2 * pw)
            u_s = us[:ms_rows].reshape(bs, ss, 2 * pw)
            ym = _pool_mixer(u_m, jnp.zeros((1, POOL_HIST, pw), F32), w_grp, pool_scale[j],
                             T=N_META, from_start=True)
            hist_s = jnp.pad(state_pool[j].astype(F32), ((0, 0), (POOL_HIST - state_pool.shape[2], 0), (0, 0)))
            ysm = _pool_mixer(u_s, hist_s, w_grp, pool_scale[j], T=ss, from_start=False)
            yp = _pool_mixer(up, u_m[:, :, :pw].astype(F32), w_grp, pool_scale[j], T=pool_tile, from_start=False)
            n_keep = state_pool.shape[2]
            pp.append(up[:, -n_keep:, :pw].astype(F32))
            s_pool.append(u_s[:, -n_keep:, :pw].astype(F32))
            w_out = pool_w_out[j]
        w_out = w_out.astype(MXU_DTYPE)
        y_small = jnp.concatenate([ysm.reshape(ms_rows, -1), ym.reshape(N_META, -1)], axis=0)
        xp = _proj_out(xp, yp.reshape(mp, -1), w_out, tm=tm_o, final_g=fin)
        xs = _proj_out(xs, y_small, w_out, tm=small, final_g=fin)

    y_prompt = xp.reshape(bp, sp, d)
    y_sample = xs[:ms_rows].reshape(bs, ss, d)
    return (y_prompt, y_sample,
            jnp.stack(pC), jnp.stack(pn), jnp.stack(pm), jnp.stack(pk), jnp.stack(pv), jnp.stack(pp),
            jnp.stack(sC), jnp.stack(sn), jnp.stack(sm), jnp.stack(sk), jnp.stack(sv), jnp.stack(s_pool))
```

```python
import functools

import numpy as np
import jax
import jax.numpy as jnp
from jax import lax
from jax.experimental import pallas as pl
from jax.experimental.pallas import tpu as pltpu

F32 = jnp.float32
MXU_DTYPE = jnp.bfloat16
ACT_DTYPE = jnp.bfloat16

EPS = 1e-6
N_META = 16
N_MIXERS = 3
PAST_LEN = 1024

ML_HEADS = 8
ML_DK = 128
ML_DV = 256
ML_QK_W = ML_HEADS * ML_DK
ML_V_W = ML_HEADS * ML_DV
ML_MAIN_W = 2 * ML_QK_W + 3 * ML_V_W
ML_GATE_PAD = 128

SW_HEADS = 32
SW_KV = 4
SW_HD = 64
SW_W = SW_HEADS * SW_HD
SW_KV_W = SW_KV * SW_HD
WINDOW = 128
SW_CHUNK = 64
ROT_DIM = 16
ROPE_THETA = 500000.0

POOL_WINDOWS = (2, 4, 8, 16)
POOL_GC = 512
POOL_HIST = 16

LANE = 128
VMEM_LIMIT_BYTES = 56 * 1024 * 1024


def _cparams(sem):
    return pltpu.CompilerParams(dimension_semantics=sem, vmem_limit_bytes=VMEM_LIMIT_BYTES)


def _nt_dot(a, b):
    return lax.dot_general(a, b, (((1,), (1,)), ((), ())), preferred_element_type=F32)


def _tn_dot(a, b):
    return lax.dot_general(a, b, (((0,), (0,)), ((), ())), preferred_element_type=F32)


def _sigmoid(x):
    return 0.5 + 0.5 * jnp.tanh(0.5 * x)


def _silu(x):
    return x * _sigmoid(x)


def _log_sigmoid(x):
    return jnp.minimum(x, 0.0) - jnp.log1p(jnp.exp(-jnp.abs(x)))


def _proj_in_kernel(*refs, rope, gates):
    x_ref, g_ref, w_ref = refs[:3]
    pos = 3
    if rope:
        ta_ref, tb_ref, tc_ref, rm_ref = refs[pos:pos + 4]
        pos += 4
    if gates:
        wg_ref, wgt_ref = refs[pos:pos + 2]
        pos += 2
    o_ref = refs[pos]
    pos += 1
    if gates:
        gc_ref, gt_ref = refs[pos:pos + 2]
        pos += 2
    hn_sc = refs[pos]
    j = pl.program_id(1)

    @pl.when(j == 0)
    def _():
        x = x_ref[...]
        ms = jnp.mean(x * x, axis=1, keepdims=True)
        hn = (x * lax.rsqrt(ms + EPS) * g_ref[...]).astype(MXU_DTYPE)
        hn_sc[...] = hn
        if gates:
            gcol = jnp.dot(hn, wg_ref[...], preferred_element_type=F32)
            gc_ref[...] = gcol[:, :2 * ML_HEADS]
            gt_ref[...] = _nt_dot(wgt_ref[...], hn)

    acc = jnp.dot(hn_sc[...], w_ref[...], preferred_element_type=F32)
    if not rope:
        o_ref[...] = acc.astype(o_ref.dtype)
        return
    has_rot = functools.reduce(jnp.logical_or, [j == t for t in rope])

    @pl.when(has_rot)
    def _():
        tn = acc.shape[1]
        reps = tn // LANE
        ta = jnp.concatenate([ta_ref[...]] * reps, axis=1)
        tb = jnp.concatenate([tb_ref[...]] * reps, axis=1)
        tc = jnp.concatenate([tc_ref[...]] * reps, axis=1)
        up = pltpu.roll(acc, tn - ROT_DIM // 2, 1)
        dn = pltpu.roll(acc, ROT_DIM // 2, 1)
        rot = acc * ta + up * tb + dn * tc
        o_ref[...] = jnp.where(rm_ref[...] > 0.0, rot, acc).astype(o_ref.dtype)

    @pl.when(jnp.logical_not(has_rot))
    def _():
        o_ref[...] = acc.astype(o_ref.dtype)


def _proj_in(x, g, w, *, tm, tn, rope_tabs=None, rope_mask=None, gate_w=None):
    m, d = x.shape
    n = w.shape[1]
    rope = () if rope_tabs is None else tuple(
        t for t in range(n // tn) if rope_mask[0, t * tn:(t + 1) * tn].any())
    gates = gate_w is not None
    grid = (m // tm, n // tn)
    in_specs = [pl.BlockSpec((tm, d), lambda i, j: (i, 0)),
                pl.BlockSpec((1, d), lambda i, j: (0, 0)),
                pl.BlockSpec((d, tn), lambda i, j: (0, j))]
    args = [x, g.reshape(1, d), w]
    if rope:
        nrep = rope_tabs[0].shape[0] // tm
        for t in rope_tabs:
            in_specs.append(pl.BlockSpec((tm, LANE), lambda i, j, nrep=nrep: (i % nrep, 0)))
            args.append(t)
        in_specs.append(pl.BlockSpec((1, tn), lambda i, j: (0, j)))
        args.append(jnp.asarray(rope_mask, F32))
    if gates:
        wg, wgt = gate_w
        in_specs.append(pl.BlockSpec((d, ML_GATE_PAD), lambda i, j: (0, 0)))
        in_specs.append(pl.BlockSpec((2 * ML_HEADS, d), lambda i, j: (0, 0)))
        args += [wg, wgt]
    out_shape = [jax.ShapeDtypeStruct((m, n), ACT_DTYPE)]
    out_specs = [pl.BlockSpec((tm, tn), lambda i, j: (i, j))]
    if gates:
        out_shape += [jax.ShapeDtypeStruct((m, 2 * ML_HEADS), F32),
                      jax.ShapeDtypeStruct((2 * ML_HEADS, m), F32)]
        out_specs += [pl.BlockSpec((tm, 2 * ML_HEADS), lambda i, j: (i, 0)),
                      pl.BlockSpec((2 * ML_HEADS, tm), lambda i, j: (0, i))]
    res = pl.pallas_call(
        functools.partial(_proj_in_kernel, rope=rope, gates=gates),
        out_shape=out_shape, grid=grid, in_specs=in_specs, out_specs=out_specs,
        scratch_shapes=[pltpu.VMEM((tm, d), MXU_DTYPE)],
        compiler_params=_cparams(("parallel", "arbitrary")),
        name="proj_in",
    )(*args)
    return res if gates else res[0]


def _proj_out_kernel(*refs, final_norm):
    y_ref, w_ref, x_ref = refs[:3]
    if final_norm:
        g_ref, o_ref = refs[3:5]
    else:
        o_ref = refs[3]
    acc = x_ref[...] + jnp.dot(y_ref[...].astype(MXU_DTYPE), w_ref[...], preferred_element_type=F32)
    if final_norm:
        ms = jnp.mean(acc * acc, axis=1, keepdims=True)
        acc = acc * lax.rsqrt(ms + EPS) * g_ref[...]
    o_ref[...] = acc


def _proj_out(x, y, w, *, tm, final_g=None):
    m, d = x.shape
    k = y.shape[1]
    final_norm = final_g is not None
    in_specs = [pl.BlockSpec((tm, k), lambda i: (i, 0)),
                pl.BlockSpec((k, d), lambda i: (0, 0)),
                pl.BlockSpec((tm, d), lambda i: (i, 0))]
    args = [y, w, x]
    if final_norm:
        in_specs.append(pl.BlockSpec((1, d), lambda i: (0, 0)))
        args.append(final_g.reshape(1, d))
    return pl.pallas_call(
        functools.partial(_proj_out_kernel, final_norm=final_norm),
        out_shape=jax.ShapeDtypeStruct((m, d), F32), grid=(m // tm,),
        in_specs=in_specs, out_specs=pl.BlockSpec((tm, d), lambda i: (i, 0)),
        compiler_params=_cparams(("parallel",)),
        name="proj_out",
    )(*args)


def _mlstm_kernel(q_ref, k_ref, v_ref, o_ref, z_ref, gc_ref, gt_ref, bc_ref, br_ref, ng_ref,
                  c0_ref, n0_ref, m0_ref, y_ref, cs_ref, ns_ref, ms_ref, *, L):
    c = pl.program_id(1)

    @pl.when(c == 0)
    def _():
        cs_ref[...] = c0_ref[...]
        ns_ref[...] = n0_ref[...]
        ms_ref[...] = m0_ref[...]

    row = lax.broadcasted_iota(jnp.int32, (L, L), 0)
    col = lax.broadcasted_iota(jnp.int32, (L, L), 1)
    tril = col <= row
    scale = ML_DK ** -0.5

    gc = gc_ref[0] + bc_ref[...]
    lane16 = lax.broadcasted_iota(jnp.int32, gc.shape, 1)
    gc = jnp.where(lane16 < ML_HEADS, gc, _log_sigmoid(gc))
    gt = gt_ref[0, 0] + br_ref[...]
    sub16 = lax.broadcasted_iota(jnp.int32, gt.shape, 0)
    gt = jnp.where(sub16 < ML_HEADS, gt, _log_sigmoid(gt))

    for h in range(ML_HEADS):
        li_col = gc[:, h:h + 1]
        lf_col = gc[:, ML_HEADS + h:ML_HEADS + h + 1]
        li_row = gt[h:h + 1, :]
        lf_row = gt[ML_HEADS + h:ML_HEADS + h + 1, :]
        q = q_ref[0, :, h * ML_DK:(h + 1) * ML_DK]
        k = k_ref[0, :, h * ML_DK:(h + 1) * ML_DK]
        v = v_ref[0, :, h * ML_DV:(h + 1) * ML_DV]
        c_old = cs_ref[0, h]
        n_old = ns_ref[0, h:h + 1, :]
        m_old = ms_ref[0, h:h + 1, 0:1]

        b_col = jnp.sum(jnp.where(tril, lf_row, 0.0), axis=1, keepdims=True)
        b_row = jnp.sum(jnp.where(row <= col, lf_col, 0.0), axis=0, keepdims=True)
        dmat = jnp.where(tril, b_col - b_row + li_row, -jnp.inf)
        inter = b_col + m_old
        m_t = jnp.maximum(inter, jnp.max(dmat, axis=1, keepdims=True))
        p = jnp.exp(dmat - m_t)
        w_inter = jnp.exp(inter - m_t)
        s = p * (_nt_dot(q, k) * scale)
        qf = q.astype(F32)
        kf = k.astype(F32)
        num = w_inter * _nt_dot(q, c_old.astype(MXU_DTYPE)) + jnp.dot(
            s.astype(MXU_DTYPE), v, preferred_element_type=F32)
        den = w_inter * jnp.sum(qf * n_old, axis=1, keepdims=True) + jnp.sum(s, axis=1, keepdims=True)
        hh = num / jnp.maximum(jnp.abs(den), jnp.exp(-m_t))

        hn = hh * lax.rsqrt(jnp.mean(hh * hh, axis=1, keepdims=True) + EPS)
        hn = hn * ng_ref[:, h * ML_DV:(h + 1) * ML_DV]
        og = o_ref[0, :, h * ML_DV:(h + 1) * ML_DV].astype(F32)
        zg = z_ref[0, :, h * ML_DV:(h + 1) * ML_DV].astype(F32)
        y_ref[0, :, h * ML_DV:(h + 1) * ML_DV] = (hn * _sigmoid(og) * _silu(zg)).astype(y_ref.dtype)

        b_end = jnp.sum(lf_row, axis=1, keepdims=True)
        a_end = b_end - b_col + li_col
        m_new = jnp.maximum(b_end + m_old, jnp.max(a_end, axis=0, keepdims=True))
        w_old = jnp.exp(b_end + m_old - m_new)
        w_s = jnp.exp(a_end - m_new)
        vw = (v.astype(F32) * w_s).astype(MXU_DTYPE)
        cs_ref[0, h] = w_old * c_old + _tn_dot(vw, k) * scale
        ns_ref[0, h:h + 1, :] = w_old * n_old + jnp.sum(kf * w_s, axis=0, keepdims=True) * scale
        ms_ref[0, h:h + 1, :] = jnp.broadcast_to(m_new, (1, LANE))


def _mlstm_mixer(u, gc, gt, b_if, norm_g, c0, n0, m0, *, L):
    bsz, s, _ = u.shape
    nc = s // L
    shared = c0.shape[0] == 1
    st = (lambda b, c: (0, 0, 0, 0)) if shared else (lambda b, c: (b, 0, 0, 0))
    st3 = (lambda b, c: (0, 0, 0)) if shared else (lambda b, c: (b, 0, 0))
    qk_b = ML_QK_W
    v_b = ML_V_W
    in_specs = [
        pl.BlockSpec((1, L, qk_b), lambda b, c: (b, c, 0)),
        pl.BlockSpec((1, L, qk_b), lambda b, c: (b, c, 1)),
        pl.BlockSpec((1, L, v_b), lambda b, c: (b, c, 1)),
        pl.BlockSpec((1, L, v_b), lambda b, c: (b, c, 2)),
        pl.BlockSpec((1, L, v_b), lambda b, c: (b, c, 3)),
        pl.BlockSpec((1, L, 2 * ML_HEADS), lambda b, c: (b, c, 0)),
        pl.BlockSpec((1, 1, 2 * ML_HEADS, L), lambda b, c: (b, c, 0, 0)),
        pl.BlockSpec((1, 2 * ML_HEADS), lambda b, c: (0, 0)),
        pl.BlockSpec((2 * ML_HEADS, 1), lambda b, c: (0, 0)),
        pl.BlockSpec((1, ML_V_W), lambda b, c: (0, 0)),
        pl.BlockSpec((1, ML_HEADS, ML_DV, ML_DK), st),
        pl.BlockSpec((1, ML_HEADS, ML_DK), st3),
        pl.BlockSpec((1, ML_HEADS, LANE), st3),
    ]
    out_shape = [jax.ShapeDtypeStruct((bsz, s, ML_V_W), ACT_DTYPE),
                 jax.ShapeDtypeStruct((bsz, ML_HEADS, ML_DV, ML_DK), F32),
                 jax.ShapeDtypeStruct((bsz, ML_HEADS, ML_DK), F32),
                 jax.ShapeDtypeStruct((bsz, ML_HEADS, LANE), F32)]
    out_specs = [pl.BlockSpec((1, L, ML_V_W), lambda b, c: (b, c, 0)),
                 pl.BlockSpec((1, ML_HEADS, ML_DV, ML_DK), lambda b, c: (b, 0, 0, 0)),
                 pl.BlockSpec((1, ML_HEADS, ML_DK), lambda b, c: (b, 0, 0)),
                 pl.BlockSpec((1, ML_HEADS, LANE), lambda b, c: (b, 0, 0))]
    b_flat = b_if.reshape(2 * ML_HEADS).astype(F32)
    y, cN, nN, mN = pl.pallas_call(
        functools.partial(_mlstm_kernel, L=L),
        out_shape=out_shape, grid=(bsz, nc), in_specs=in_specs, out_specs=out_specs,
        compiler_params=_cparams(("parallel", "arbitrary")),
        name="mlstm_mixer",
    )(u, u, u, u, u, gc, gt, b_flat.reshape(1, -1), b_flat.reshape(-1, 1),
      norm_g.reshape(1, ML_V_W).astype(F32), c0, n0, m0)
    return y, cN, nN, mN[:, :, 0]


ML_ST_W = ML_DV + LANE
ML_EXT_ROWS = 16


def _split3(x):
    x1 = x.astype(jnp.bfloat16).astype(F32)
    r1 = x - x1
    x2 = r1.astype(jnp.bfloat16).astype(F32)
    x3 = (r1 - x2).astype(jnp.bfloat16).astype(F32)
    return x1, x2, x3


def _gate_scan_kernel(g_ref, br_ref, o_ref, *, L):
    H = ML_HEADS
    g = g_ref[...] + br_ref[...]
    li = g[0:H]
    pos = lax.rem(lax.broadcasted_iota(jnp.int32, li.shape, 1), L)
    b = _log_sigmoid(g[H:2 * H])
    d = 1
    while d < L:
        b = b + jnp.where(pos >= d, pltpu.roll(b, d, 1), 0.0)
        d *= 2
    beta0 = li - b
    cm = beta0
    d = 1
    while d < L:
        cm = jnp.maximum(cm, jnp.where(pos >= d, pltpu.roll(cm, d, 1), -jnp.inf))
        d *= 2
    o_ref[0:H, :] = beta0
    o_ref[H:2 * H, :] = b
    o_ref[2 * H:3 * H, :] = cm


def _gate_scan(gt, b_if, *, L, tb):
    m = gt.shape[1]
    return pl.pallas_call(
        functools.partial(_gate_scan_kernel, L=L),
        out_shape=jax.ShapeDtypeStruct((3 * ML_HEADS, m), F32), grid=(m // tb,),
        in_specs=[pl.BlockSpec((2 * ML_HEADS, tb), lambda i: (0, i)),
                  pl.BlockSpec((2 * ML_HEADS, 1), lambda i: (0, 0))],
        out_specs=pl.BlockSpec((3 * ML_HEADS, tb), lambda i: (0, i)),
        compiler_params=_cparams(("parallel",)),
        name="mlstm_gate_scan",
    )(gt, b_if.reshape(2 * ML_HEADS, 1).astype(F32))


def _mlstm_long_kernel(q_ref, k_ref, v_ref, o_ref, z_ref, gs_ref, ng_ref, c0_ref, n0_ref, m0_ref,
                       y_ref, cs_ref, ns_ref, ms_ref, st_sc, *, L):
    c = pl.program_id(1)
    nc = pl.num_programs(1)
    H = ML_HEADS
    bf = jnp.bfloat16

    @pl.when(c == 0)
    def _():
        for h in range(H):
            st_sc[h, :, 0:ML_DV] = c0_ref[0, h].T
            st_sc[h, :, ML_DV:ML_ST_W] = jnp.broadcast_to(n0_ref[0, h:h + 1, :], (ML_DK, LANE)).T
        ms_ref[...] = m0_ref[...]

    row = lax.broadcasted_iota(jnp.int32, (L, L), 0)
    col = lax.broadcasted_iota(jnp.int32, (L, L), 1)
    tril = col <= row

    beta0 = gs_ref[0:H, :]
    b = gs_ref[H:2 * H, :]
    cm = gs_ref[2 * H:3 * H, :]
    m_old = ms_ref[0]
    mm = jnp.maximum(jnp.concatenate([m_old] * (L // LANE), axis=1), cm)
    a3 = _split3(-mm)
    b3 = _split3(beta0 + jnp.log(jnp.float32(ML_DK ** -0.5)))
    n3 = _split3(-b)
    o3 = _split3(m_old)

    ones_l = jnp.ones((3, L), F32)
    ones_s = jnp.ones((3, LANE), F32)
    sub = lax.broadcasted_iota(jnp.int32, (ML_EXT_ROWS, LANE), 0)
    rhs_f = jnp.where((sub < 3) | ((sub >= 6) & (sub < 9)), 1.0, 0.0)
    ones_v = jnp.ones((L, LANE), MXU_DTYPE)
    mean_w = jnp.full((ML_DV, LANE), 1.0 / ML_DV, MXU_DTYPE)

    for h in range(H):
        hs = slice(h, h + 1)
        lhs = jnp.concatenate([a3[0][hs], a3[1][hs], a3[2][hs], ones_l, n3[0][hs], n3[1][hs], n3[2][hs],
                               jnp.zeros((ML_EXT_ROWS - 9, L), F32)], axis=0)
        rhs_e = jnp.concatenate([ones_l, b3[0][hs], b3[1][hs], b3[2][hs],
                                 jnp.zeros((ML_EXT_ROWS - 6, L), F32)], axis=0)
        rhs_g = jnp.concatenate([ones_s, o3[0][hs], o3[1][hs], o3[2][hs],
                                 jnp.zeros((ML_EXT_ROWS - 6, LANE), F32)], axis=0)
        rhs = jnp.concatenate([rhs_g, rhs_f, rhs_e], axis=1).astype(bf)
        ext = _tn_dot(lhs.astype(bf), rhs)
        ef = ext[:, LANE:2 * LANE]
        wg = jnp.exp(ext[:, 0:LANE])
        fl = jnp.exp(ef)
        p = jnp.exp(jnp.where(tril, ext[:, 2 * LANE:], -jnp.inf))

        q = q_ref[0, :, h * ML_DK:(h + 1) * ML_DK]
        k = k_ref[0, :, h * ML_DK:(h + 1) * ML_DK]
        v = v_ref[0, :, h * ML_DV:(h + 1) * ML_DV]
        s = p * _nt_dot(q, k)
        qw = q.astype(F32) * wg
        x = jnp.concatenate([qw.astype(MXU_DTYPE), s.astype(MXU_DTYPE)], axis=1)
        vo = jnp.concatenate([v.astype(MXU_DTYPE), ones_v], axis=1)
        st = st_sc[h]
        res = jnp.dot(x, jnp.concatenate([st.astype(MXU_DTYPE), vo], axis=0), preferred_element_type=F32)
        r = 1.0 / jnp.maximum(jnp.abs(res[:, ML_DV:]), fl)
        hh = res[:, :ML_DV] * jnp.concatenate([r, r], axis=1)
        msq = jnp.dot((hh * hh).astype(MXU_DTYPE), mean_w, preferred_element_type=F32)
        rs = lax.rsqrt(msq + EPS)
        hn = hh * jnp.concatenate([rs, rs], axis=1) * ng_ref[:, h * ML_DV:(h + 1) * ML_DV]
        og = o_ref[0, :, h * ML_DV:(h + 1) * ML_DV].astype(F32)
        zg = z_ref[0, :, h * ML_DV:(h + 1) * ML_DV].astype(F32)
        gate = (0.25 * zg) * (1.0 + jnp.tanh(0.5 * og)) * (1.0 + jnp.tanh(0.5 * zg))
        y_ref[0, :, h * ML_DV:(h + 1) * ML_DV] = (hn * gate).astype(y_ref.dtype)

        ktw = (k.astype(F32).T * p[L - 1:L, :]).astype(MXU_DTYPE)
        w_old = wg[L - 1:L, :]
        st_sc[h] = st * jnp.concatenate([w_old] * (ML_ST_W // LANE), axis=1) + jnp.dot(
            ktw, vo, preferred_element_type=F32)
        ms_ref[0, hs, :] = -ef[L - 1:L, :]

    @pl.when(c == nc - 1)
    def _():
        for h in range(H):
            cs_ref[0, h] = st_sc[h, :, 0:ML_DV].T
            ns_ref[0, h:h + 1, :] = st_sc[h, :, ML_DV:ML_ST_W].T[0:1, :]


def _mlstm_mixer_long(u, gscan, norm_g, c0, n0, m0, *, L):
    bsz, s, _ = u.shape
    nc = s // L
    shared = c0.shape[0] == 1
    st = (lambda b, c: (0, 0, 0, 0)) if shared else (lambda b, c: (b, 0, 0, 0))
    st3 = (lambda b, c: (0, 0, 0)) if shared else (lambda b, c: (b, 0, 0))
    in_specs = [
        pl.BlockSpec((1, L, ML_QK_W), lambda b, c: (b, c, 0)),
        pl.BlockSpec((1, L, ML_QK_W), lambda b, c: (b, c, 1)),
        pl.BlockSpec((1, L, ML_V_W), lambda b, c: (b, c, 1)),
        pl.BlockSpec((1, L, ML_V_W), lambda b, c: (b, c, 2)),
        pl.BlockSpec((1, L, ML_V_W), lambda b, c: (b, c, 3)),
        pl.BlockSpec((3 * ML_HEADS, L), lambda b, c, nc=nc: (0, b * nc + c)),
        pl.BlockSpec((1, ML_V_W), lambda b, c: (0, 0)),
        pl.BlockSpec((1, ML_HEADS, ML_DV, ML_DK), st),
        pl.BlockSpec((1, ML_HEADS, ML_DK), st3),
        pl.BlockSpec((1, ML_HEADS, LANE), st3),
    ]
    out_shape = [jax.ShapeDtypeStruct((bsz, s, ML_V_W), ACT_DTYPE),
                 jax.ShapeDtypeStruct((bsz, ML_HEADS, ML_DV, ML_DK), F32),
                 jax.ShapeDtypeStruct((bsz, ML_HEADS, ML_DK), F32),
                 jax.ShapeDtypeStruct((bsz, ML_HEADS, LANE), F32)]
    out_specs = [pl.BlockSpec((1, L, ML_V_W), lambda b, c: (b, c, 0)),
                 pl.BlockSpec((1, ML_HEADS, ML_DV, ML_DK), lambda b, c: (b, 0, 0, 0)),
                 pl.BlockSpec((1, ML_HEADS, ML_DK), lambda b, c: (b, 0, 0)),
                 pl.BlockSpec((1, ML_HEADS, LANE), lambda b, c: (b, 0, 0))]
    y, cN, nN, mN = pl.pallas_call(
        functools.partial(_mlstm_long_kernel, L=L),
        out_shape=out_shape, grid=(bsz, nc), in_specs=in_specs, out_specs=out_specs,
        scratch_shapes=[pltpu.VMEM((ML_HEADS, ML_DK, ML_ST_W), F32)],
        compiler_params=_cparams(("parallel", "arbitrary")),
        name="mlstm_mixer_long",
    )(u, u, u, u, u, gscan, norm_g.reshape(1, ML_V_W).astype(F32), c0, n0, m0)
    return y, cN, nN, mN[:, :, 0]


LOG2E = 1.4426950408889634


def _swa_kernel(*refs, Lc, G, n_hist, n_invalid):
    sink_ref, q_ref, z_ref, kv_ref = refs[:4]
    pos = 4
    if n_hist:
        hist_ref = refs[pos]
        prev_refs = refs[pos + 1:pos + 1 + n_hist]
        pos += 1 + n_hist
    y_ref = refs[pos]
    c = pl.program_id(1)
    nk = (n_hist + 1) * Lc
    pairs = SW_HEADS // SW_KV // 2
    rows = pairs * Lc

    kv = kv_ref[0]
    if n_hist:
        prev = jnp.concatenate([r[0] for r in prev_refs], axis=0)
        kv = jnp.concatenate([jnp.where(c == 0, hist_ref[0], prev), kv], axis=0)
    kv = kv.astype(F32)
    nkr = kv.shape[0]
    lane = lax.broadcasted_iota(jnp.int32, (nkr, LANE), 1)
    low = lane < SW_HD
    blk = lax.broadcasted_iota(jnp.int32, (rows, 1), 0) // Lc
    ones_v = jnp.ones((nkr, LANE), MXU_DTYPE)

    def split(t, g):
        swapped = pltpu.roll(t, SW_HD, 1)
        if g % 2 == 0:
            lo, hi = t, swapped
        else:
            lo, hi = swapped, t
        return (jnp.where(low, lo, 0.0).astype(MXU_DTYPE), jnp.where(low, 0.0, hi).astype(MXU_DTYPE))

    for g in range(SW_KV):
        kt = kv[:, (g // 2) * LANE:(g // 2 + 1) * LANE] * (SW_HD ** -0.5 * LOG2E)
        vt = kv[:, SW_KV_W + (g // 2) * LANE:SW_KV_W + (g // 2 + 1) * LANE]
        k_par = split(kt, g)
        v_par = tuple(jnp.concatenate([vv, ones_v], axis=1) for vv in split(vt, g))
        sinks = []
        for par in range(2):
            sk = jnp.zeros((rows, 1), F32)
            for j in range(pairs):
                sk = jnp.where(blk == j, sink_ref[SW_HEADS // SW_KV * g + 2 * j + par] * LOG2E, sk)
            sinks.append(sk)
        for i in range(G):
            ks = slice(i * Lc, i * Lc + nk)
            qs = jnp.concatenate([q_ref[0, i * Lc:(i + 1) * Lc, (pairs * g + j) * LANE:(pairs * g + j + 1) * LANE]
                                  for j in range(pairs)], axis=0)
            out = jnp.zeros((rows, LANE), F32)
            for par in range(2):
                s = _nt_dot(qs, k_par[par][ks])
                if n_invalid > i * Lc:
                    key_idx = (c * G + i) * Lc + lax.broadcasted_iota(jnp.int32, (1, nk), 1)
                    s = jnp.where(key_idx >= n_invalid, s, -jnp.inf)
                mx = jnp.maximum(jnp.max(s, axis=1, keepdims=True), sinks[par])
                p = jnp.exp2(s - mx)
                res = jnp.dot(p.astype(MXU_DTYPE), v_par[par][ks], preferred_element_type=F32)
                den = res[:, LANE:] + jnp.exp2(sinks[par] - mx)
                out = out + res[:, :LANE] / den
            for j in range(pairs):
                cs = slice((pairs * g + j) * LANE, (pairs * g + j + 1) * LANE)
                zg = z_ref[0, i * Lc:(i + 1) * Lc, cs].astype(F32)
                y_ref[0, i * Lc:(i + 1) * Lc, cs] = (out[j * Lc:(j + 1) * Lc] * _silu(zg)).astype(y_ref.dtype)


def _swa_mixer(u, hist, sinks, *, Lc, G, n_invalid):
    bsz, s, _ = u.shape
    tq = G * Lc
    nc = s // tq
    kv_blk = 2 * SW_W // (2 * SW_KV_W)
    n_hist = 0 if hist is None else hist.shape[1] // Lc
    in_specs = [pl.BlockSpec(memory_space=pltpu.SMEM),
                pl.BlockSpec((1, tq, SW_W), lambda b, c: (b, c, 0)),
                pl.BlockSpec((1, tq, SW_W), lambda b, c: (b, c, 1)),
                pl.BlockSpec((1, tq, 2 * SW_KV_W), lambda b, c: (b, c, kv_blk))]
    args = [sinks.astype(F32), u, u, u]
    if n_hist:
        hmap = (lambda b, c: (0, 0, 0)) if hist.shape[0] == 1 else (lambda b, c: (b, 0, 0))
        in_specs.append(pl.BlockSpec((1, n_hist * Lc, 2 * SW_KV_W), hmap))
        args.append(hist)
        for i in range(n_hist):
            in_specs.append(pl.BlockSpec(
                (1, Lc, 2 * SW_KV_W),
                lambda b, c, i=i: (b, jnp.maximum(c * G - n_hist + i, 0), kv_blk)))
            args.append(u)
    return pl.pallas_call(
        functools.partial(_swa_kernel, Lc=Lc, G=G, n_hist=n_hist, n_invalid=n_invalid),
        out_shape=jax.ShapeDtypeStruct((bsz, s, SW_W), ACT_DTYPE), grid=(bsz, nc),
        in_specs=in_specs, out_specs=pl.BlockSpec((1, tq, SW_W), lambda b, c: (b, c, 0)),
        compiler_params=_cparams(("parallel", "parallel")),
        name="swa_mixer",
    )(*args)


def _pool_kernel(u_ref, z_ref, h0_ref, wg_ref, sc_ref, y_ref, e_sc, *, T, from_start):
    t = pl.program_id(1)

    @pl.when(t == 0)
    def _():
        e_sc[0:POOL_HIST, :] = h0_ref[0]

    u = u_ref[0].astype(F32)
    e_sc[POOL_HIST:POOL_HIST + T, :] = u
    for g, w in enumerate(POOL_WINDOWS):
        cs = slice(g * POOL_GC, (g + 1) * POOL_GC)
        ug = u[:, cs]
        acc = ug
        for d in range(1, w):
            acc = acc + e_sc[POOL_HIST - d:POOL_HIST - d + T, cs]
        if from_start:
            pos = t * T + lax.broadcasted_iota(jnp.int32, (T, 1), 0) + 1
            pooled = acc / jnp.minimum(pos, w).astype(F32)
        else:
            pooled = acc * (1.0 / w)
        pooled = pooled - ug
        mixed = jnp.dot(pooled.astype(MXU_DTYPE), wg_ref[g], preferred_element_type=F32)
        zg = z_ref[0, :, cs].astype(F32)
        y_ref[0, :, cs] = (mixed * sc_ref[:, cs] * _silu(zg)).astype(y_ref.dtype)
    e_sc[0:POOL_HIST, :] = e_sc[T:T + POOL_HIST, :]


def _pool_mixer(u, hist, w_grp, scale, *, T, from_start):
    bsz, s, w2 = u.shape
    w = w2 // 2
    nt = s // T
    hmap = (lambda b, t: (0, 0, 0)) if hist.shape[0] == 1 else (lambda b, t: (b, 0, 0))
    return pl.pallas_call(
        functools.partial(_pool_kernel, T=T, from_start=from_start),
        out_shape=jax.ShapeDtypeStruct((bsz, s, w), ACT_DTYPE), grid=(bsz, nt),
        in_specs=[pl.BlockSpec((1, T, w), lambda b, t: (b, t, 0)),
                  pl.BlockSpec((1, T, w), lambda b, t: (b, t, 1)),
                  pl.BlockSpec((1, POOL_HIST, w), hmap),
                  pl.BlockSpec((len(POOL_WINDOWS), POOL_GC, POOL_GC), lambda b, t: (0, 0, 0)),
                  pl.BlockSpec((1, w), lambda b, t: (0, 0))],
        out_specs=pl.BlockSpec((1, T, w), lambda b, t: (b, t, 0)),
        scratch_shapes=[pltpu.VMEM((POOL_HIST + T, w), F32)],
        compiler_params=_cparams(("parallel", "arbitrary")),
        name="pool_mixer",
    )(u, u, hist, w_grp, scale.reshape(1, w).astype(F32))


def _pool_band_matrices(T):
    t = np.arange(T)[:, None]
    k = np.arange(T)[None, :]
    th = np.arange(POOL_HIST)[:, None]
    ph = np.arange(POOL_HIST)[None, :] - POOL_HIST
    main = [((k <= t) & (k > t - w)) / w - (k == t) for w in POOL_WINDOWS]
    hist = [(ph > th - w) / w for w in POOL_WINDOWS]
    return np.stack(main).astype(np.float32), np.stack(hist).astype(np.float32)


def _pool_long_kernel(u_ref, z_ref, h0_ref, wg_ref, sc_ref, bm_ref, bh_ref, y_ref, hist_sc, *, T):
    t = pl.program_id(1)

    @pl.when(t == 0)
    def _():
        hist_sc[...] = h0_ref[0]

    for g in range(len(POOL_WINDOWS)):
        cs = slice(g * POOL_GC, (g + 1) * POOL_GC)
        pooled = jnp.dot(bm_ref[g], u_ref[0, :, cs].astype(MXU_DTYPE), preferred_element_type=F32)
        head = pooled[0:POOL_HIST] + jnp.dot(bh_ref[g], hist_sc[:, cs].astype(MXU_DTYPE),
                                             preferred_element_type=F32)
        pooled = jnp.concatenate([head, pooled[POOL_HIST:]], axis=0)
        mixed = jnp.dot(pooled.astype(MXU_DTYPE), wg_ref[g], preferred_element_type=F32)
        zg = z_ref[0, :, cs].astype(F32)
        y_ref[0, :, cs] = (mixed * sc_ref[:, cs] * _silu(zg)).astype(y_ref.dtype)
    hist_sc[...] = u_ref[0, T - POOL_HIST:T, 0:hist_sc.shape[1]]


def _pool_mixer_long(u, hist, w_grp, scale, *, T):
    assert jnp.dtype(ACT_DTYPE).itemsize <= jnp.dtype(MXU_DTYPE).itemsize
    bsz, s, w2 = u.shape
    w = w2 // 2
    nt = s // T
    ng = len(POOL_WINDOWS)
    bm, bh = _pool_band_matrices(T)
    hmap = (lambda b, t: (0, 0, 0)) if hist.shape[0] == 1 else (lambda b, t: (b, 0, 0))
    return pl.pallas_call(
        functools.partial(_pool_long_kernel, T=T),
        out_shape=jax.ShapeDtypeStruct((bsz, s, w), ACT_DTYPE), grid=(bsz, nt),
        in_specs=[pl.BlockSpec((1, T, w), lambda b, t: (b, t, 0)),
                  pl.BlockSpec((1, T, w), lambda b, t: (b, t, 1)),
                  pl.BlockSpec((1, POOL_HIST, w), hmap),
                  pl.BlockSpec((ng, POOL_GC, POOL_GC), lambda b, t: (0, 0, 0)),
                  pl.BlockSpec((1, w), lambda b, t: (0, 0)),
                  pl.BlockSpec((ng, T, T), lambda b, t: (0, 0, 0)),
                  pl.BlockSpec((ng, POOL_HIST, POOL_HIST), lambda b, t: (0, 0, 0))],
        out_specs=pl.BlockSpec((1, T, w), lambda b, t: (b, t, 0)),
        scratch_shapes=[pltpu.VMEM((POOL_HIST, w), ACT_DTYPE)],
        compiler_params=_cparams(("parallel", "arbitrary")),
        name="pool_mixer_long",
    )(u, u, hist, w_grp, scale.reshape(1, w).astype(F32),
      jnp.asarray(bm, MXU_DTYPE), jnp.asarray(bh, MXU_DTYPE))


def _rope_tables(pos):
    half = ROT_DIM // 2
    inv = jnp.power(ROPE_THETA, -jnp.arange(half, dtype=F32) / half)
    ang = pos.astype(F32)[:, None] * inv[None, :]
    cos, sin = jnp.cos(ang), jnp.sin(ang)
    n = pos.shape[0]
    pad = jnp.zeros((n, SW_HD - ROT_DIM), F32)
    ta = jnp.concatenate([cos, cos, pad + 1.0], axis=1)
    tb = jnp.concatenate([-sin, jnp.zeros((n, half), F32), pad], axis=1)
    tc = jnp.concatenate([jnp.zeros((n, half), F32), sin, pad], axis=1)
    return tuple(jnp.concatenate([t, t], axis=1) for t in (ta, tb, tc))


def _pick_tile(m, pref):
    t = pref
    while m % t:
        t //= 2
    return t


def kernel(x_prompt, x_sample, state_mlstm_C, state_mlstm_n, state_mlstm_m, cache_swa_k, cache_swa_v, state_pool, meta_tokens, norm_g, final_norm_g, mlstm_w_in, mlstm_b_if, mlstm_norm_g, mlstm_w_out, swa_w_in, swa_sinks, swa_w_out, pool_w_in, pool_w_grp, pool_scale, pool_w_out):
    bp, sp, d = x_prompt.shape
    bs, ss, _ = x_sample.shape
    depth = norm_g.shape[0]
    mp = bp * sp
    ms_rows = bs * ss
    small = ms_rows + N_META

    xp = x_prompt.reshape(mp, d)
    xs = jnp.concatenate([x_sample.reshape(ms_rows, d), meta_tokens.astype(x_prompt.dtype)], axis=0)
    tm_p = _pick_tile(sp, 1024)
    tm_o = _pick_tile(sp, 512)
    ml_chunk = _pick_tile(sp, 256)
    pool_tile = _pick_tile(sp, 256)

    pC, pn, pm, pk, pv, pp = [], [], [], [], [], []
    sC, sn, sm, sk, sv, s_pool = [], [], [], [], [], []
    for i in range(depth):
        kind, j = i % N_MIXERS, i // N_MIXERS
        last = i == depth - 1
        fin = final_norm_g if last else None
        if kind == 0:
            w_in = mlstm_w_in[j]
            w_main = w_in[:, :ML_MAIN_W].astype(MXU_DTYPE)
            wg = w_in[:, ML_MAIN_W:]
            wg_pad = jnp.pad(wg, ((0, 0), (0, ML_GATE_PAD - 2 * ML_HEADS))).astype(MXU_DTYPE)
            gate_w = (wg_pad, wg.T.astype(MXU_DTYPE))
            up, gcp, gtp = _proj_in(xp, norm_g[i], w_main, tm=tm_p, tn=1024, gate_w=gate_w)
            us, gcs, gts = _proj_in(xs, norm_g[i], w_main, tm=small, tn=1024, gate_w=gate_w)
            zc = jnp.zeros((1, ML_HEADS, ML_DV, ML_DK), F32)
            zn = jnp.zeros((1, ML_HEADS, ML_DK), F32)
            zm = jnp.zeros((1, ML_HEADS, LANE), F32)
            ym, c_m, n_m, m_m = _mlstm_mixer(
                us[ms_rows:].reshape(1, N_META, -1), gcs[ms_rows:].reshape(1, N_META, -1),
                gts[:, ms_rows:].reshape(1, 1, 2 * ML_HEADS, N_META),
                mlstm_b_if[j], mlstm_norm_g[j], zc, zn, zm, L=N_META)
            ysm, c_s, n_s, m_s = _mlstm_mixer(
                us[:ms_rows].reshape(bs, ss, -1), gcs[:ms_rows].reshape(bs, ss, -1),
                gts[:, :ms_rows].reshape(2 * ML_HEADS, bs, 1, ss).transpose(1, 2, 0, 3),
                mlstm_b_if[j], mlstm_norm_g[j], state_mlstm_C[j].astype(F32), state_mlstm_n[j].astype(F32),
                jnp.broadcast_to(state_mlstm_m[j].astype(F32)[..., None], (bs, ML_HEADS, LANE)), L=ss)
            yp, c_p, n_p, m_p = _mlstm_mixer_long(
                up.reshape(bp, sp, -1), _gate_scan(gtp, mlstm_b_if[j], L=ml_chunk, tb=sp),
                mlstm_norm_g[j], c_m, n_m,
                jnp.broadcast_to(m_m[..., None], (1, ML_HEADS, LANE)), L=ml_chunk)
            pC.append(c_p); pn.append(n_p); pm.append(m_p)
            sC.append(c_s); sn.append(n_s); sm.append(m_s)
            w_out = mlstm_w_out[j]
        elif kind == 1:
            w_in = swa_w_in[j]
            wq, wk, wv, wz = jnp.split(w_in, [SW_W, SW_W + SW_KV_W, SW_W + 2 * SW_KV_W], axis=1)
            w_perm = jnp.concatenate([wq, wz, wk, wv], axis=1).astype(MXU_DTYPE)
            n_in = w_perm.shape[1]
            rope_mask = np.concatenate([np.ones((1, SW_W)), np.zeros((1, SW_W)),
                                        np.ones((1, SW_KV_W)), np.zeros((1, SW_KV_W))], axis=1)
            tabs_p = _rope_tables(N_META + jnp.arange(sp))
            pos_s = jnp.concatenate([jnp.tile(N_META + PAST_LEN + jnp.arange(ss), bs), jnp.arange(N_META)])
            tabs_s = _rope_tables(pos_s)
            up = _proj_in(xp, norm_g[i], w_perm, tm=tm_p, tn=512, rope_tabs=tabs_p, rope_mask=rope_mask)
            us = _proj_in(xs, norm_g[i], w_perm, tm=small, tn=512, rope_tabs=tabs_s, rope_mask=rope_mask)
            up = up.reshape(bp, sp, n_in)
            kv_s = us[:ms_rows, 2 * SW_W:].reshape(bs, ss, 2 * SW_KV_W)
            kv_m = us[ms_rows:, 2 * SW_W:].reshape(1, N_META, 2 * SW_KV_W)
            ym = _swa_mixer(us[ms_rows:].reshape(1, N_META, n_in), None, swa_sinks[j],
                            Lc=N_META, G=1, n_invalid=0)
            cache = jnp.concatenate([cache_swa_k[j].reshape(bs, WINDOW, SW_KV_W),
                                     cache_swa_v[j].reshape(bs, WINDOW, SW_KV_W)], axis=-1).astype(ACT_DTYPE)
            ysm = _swa_mixer(us[:ms_rows].reshape(bs, ss, n_in), cache, swa_sinks[j],
                             Lc=SW_CHUNK, G=ss // SW_CHUNK, n_invalid=0)
            hist = jnp.concatenate([jnp.zeros((1, WINDOW - N_META, 2 * SW_KV_W), ACT_DTYPE), kv_m], axis=1)
            yp = _swa_mixer(up, hist, swa_sinks[j], Lc=SW_CHUNK, G=_pick_tile(sp // SW_CHUNK, 4),
                            n_invalid=WINDOW - N_META)
            kv_p = up[:, -WINDOW:, 2 * SW_W:]
            pk.append(kv_p[:, :, :SW_KV_W].astype(F32).reshape(bp, WINDOW, SW_KV, SW_HD))
            pv.append(kv_p[:, :, SW_KV_W:].astype(F32).reshape(bp, WINDOW, SW_KV, SW_HD))
            k_new = kv_s[:, :, :SW_KV_W].astype(F32).reshape(bs, ss, SW_KV, SW_HD)
            v_new = kv_s[:, :, SW_KV_W:].astype(F32).reshape(bs, ss, SW_KV, SW_HD)
            sk.append(jnp.concatenate([cache_swa_k[j].astype(F32), k_new], axis=1)[:, -WINDOW:])
            sv.append(jnp.concatenate([cache_swa_v[j].astype(F32), v_new], axis=1)[:, -WINDOW:])
            w_out = swa_w_out[j]
        else:
            w_in = pool_w_in[j].astype(MXU_DTYPE)
            pw = w_in.shape[1] // 2
            w_grp = pool_w_grp[j].astype(MXU_DTYPE)
            up = _proj_in(xp, norm_g[i], w_in, tm=tm_p, tn=1024).reshape(bp, sp, 2 * pw)
            us = _proj_in(xs, norm_g[i], w_in, tm=small, tn=1024)
            u_m = us[ms_rows:].reshape(1, N_META, 2 * pw)
            u_s = us[:ms_rows].reshape(bs, ss, 2 * pw)
            ym = _pool_mixer(u_m, jnp.zeros((1, POOL_HIST, pw), F32), w_grp, pool_scale[j],
                             T=N_META, from_start=True)
            hist_s = jnp.pad(state_pool[j].astype(F32), ((0, 0), (POOL_HIST - state_pool.shape[2], 0), (0, 0)))
            ysm = _pool_mixer(u_s, hist_s, w_grp, pool_scale[j], T=ss, from_start=False)
            yp = _pool_mixer_long(up, u_m[:, :, :pw], w_grp, pool_scale[j], T=pool_tile)
            n_keep = state_pool.shape[2]
            pp.append(up[:, -n_keep:, :pw].astype(F32))
            s_pool.append(u_s[:, -n_keep:, :pw].astype(F32))
            w_out = pool_w_out[j]
        w_out = w_out.astype(MXU_DTYPE)
        y_small = jnp.concatenate([ysm.reshape(ms_rows, -1), ym.reshape(N_META, -1)], axis=0)
        xp = _proj_out(xp, yp.reshape(mp, -1), w_out, tm=tm_o, final_g=fin)
        xs = _proj_out(xs, y_small, w_out, tm=small, final_g=fin)

    y_prompt = xp.reshape(bp, sp, d)
    y_sample = xs[:ms_rows].reshape(bs, ss, d)
    return (y_prompt, y_sample,
            jnp.stack(pC), jnp.stack(pn), jnp.stack(pm), jnp.stack(pk), jnp.stack(pv), jnp.stack(pp),
            jnp.stack(sC), jnp.stack(sn), jnp.stack(sm), jnp.stack(sk), jnp.stack(sv), jnp.stack(s_pool))
```

```python
import functools

import numpy as np
import jax
import jax.numpy as jnp
from jax import lax
from jax.experimental import pallas as pl
from jax.experimental.pallas import tpu as pltpu

F32 = jnp.float32
MXU_DTYPE = jnp.bfloat16
ACT_DTYPE = jnp.bfloat16

EPS = 1e-6
N_META = 16
N_MIXERS = 3
PAST_LEN = 1024

ML_HEADS = 8
ML_DK = 128
ML_DV = 256
ML_QK_W = ML_HEADS * ML_DK
ML_V_W = ML_HEADS * ML_DV
ML_MAIN_W = 2 * ML_QK_W + 3 * ML_V_W
ML_GATE_PAD = 128

SW_HEADS = 32
SW_KV = 4
SW_HD = 64
SW_W = SW_HEADS * SW_HD
SW_KV_W = SW_KV * SW_HD
WINDOW = 128
SW_CHUNK = 64
ROT_DIM = 16
ROPE_THETA = 500000.0

POOL_WINDOWS = (2, 4, 8, 16)
POOL_GC = 512
POOL_HIST = 16

LANE = 128
VMEM_LIMIT_BYTES = 56 * 1024 * 1024


def _cparams(sem):
    return pltpu.CompilerParams(dimension_semantics=sem, vmem_limit_bytes=VMEM_LIMIT_BYTES)


def _nt_dot(a, b):
    return lax.dot_general(a, b, (((1,), (1,)), ((), ())), preferred_element_type=F32)


def _tn_dot(a, b):
    return lax.dot_general(a, b, (((0,), (0,)), ((), ())), preferred_element_type=F32)


def _sigmoid(x):
    return 0.5 + 0.5 * jnp.tanh(0.5 * x)


def _silu(x):
    return x * _sigmoid(x)


def _log_sigmoid(x):
    return jnp.minimum(x, 0.0) - jnp.log1p(jnp.exp(-jnp.abs(x)))


def _proj_in_kernel(*refs, rope, gates, w_transposed):
    x_ref, g_ref, w_ref = refs[:3]
    pos = 3
    if rope:
        ta_ref, ts_ref, rm_ref, perm_ref = refs[pos:pos + 4]
        pos += 4
    if gates:
        wg_ref = refs[pos]
        pos += 1
    o_ref = refs[pos]
    pos += 1
    if gates:
        gc_ref, gt_ref = refs[pos:pos + 2]
        pos += 2
    hn_sc = refs[pos]
    j = pl.program_id(1)

    @pl.when(j == 0)
    def _():
        x = x_ref[...]
        ms = jnp.mean(x * x, axis=1, keepdims=True)
        hn = (x * lax.rsqrt(ms + EPS) * g_ref[...]).astype(MXU_DTYPE)
        hn_sc[...] = hn
        if gates:
            gcol = jnp.dot(hn, wg_ref[...], preferred_element_type=F32)
            gc_ref[...] = gcol[:, :2 * ML_HEADS]
            gt_ref[...] = gcol.T[:2 * ML_HEADS, :]

    if w_transposed:
        acc = _nt_dot(hn_sc[...], w_ref[...])
    else:
        acc = jnp.dot(hn_sc[...], w_ref[...], preferred_element_type=F32)
    if not rope:
        o_ref[...] = acc.astype(o_ref.dtype)
        return
    has_rot = functools.reduce(jnp.logical_or, [j == t for t in rope])

    @pl.when(has_rot)
    def _():
        reps = acc.shape[1] // LANE
        ta = jnp.concatenate([ta_ref[...]] * reps, axis=1)
        ts = jnp.concatenate([ts_ref[...]] * reps, axis=1)
        ab = acc.astype(MXU_DTYPE)
        partner = jnp.concatenate(
            [jnp.dot(ab[:, t * LANE:(t + 1) * LANE], perm_ref[...], preferred_element_type=F32)
             for t in range(reps)], axis=1)
        rot = acc * ta + partner * ts
        o_ref[...] = jnp.where(rm_ref[...] > 0.0, rot, acc).astype(o_ref.dtype)

    @pl.when(jnp.logical_not(has_rot))
    def _():
        o_ref[...] = acc.astype(o_ref.dtype)


def _proj_in(x, g, w, *, tm, tn, rope_tabs=None, rope_mask=None, gate_w=None, w_transposed=False):
    m, d = x.shape
    n = w.shape[0] if w_transposed else w.shape[1]
    rope = () if rope_tabs is None else tuple(
        t for t in range(n // tn) if rope_mask[0, t * tn:(t + 1) * tn].any())
    gates = gate_w is not None
    grid = (m // tm, n // tn)
    in_specs = [pl.BlockSpec((tm, d), lambda i, j: (i, 0)),
                pl.BlockSpec((1, d), lambda i, j: (0, 0)),
                pl.BlockSpec((tn, d), lambda i, j: (j, 0)) if w_transposed
                else pl.BlockSpec((d, tn), lambda i, j: (0, j))]
    args = [x, g.reshape(1, d), w]
    if rope:
        nrep = rope_tabs[0].shape[0] // tm
        for t in rope_tabs:
            in_specs.append(pl.BlockSpec((tm, LANE), lambda i, j, nrep=nrep: (i % nrep, 0)))
            args.append(t)
        in_specs.append(pl.BlockSpec((1, tn), lambda i, j: (0, j)))
        args.append(jnp.asarray(rope_mask, F32))
        in_specs.append(pl.BlockSpec((LANE, LANE), lambda i, j: (0, 0)))
        args.append(jnp.asarray(_rope_partner_matrix(), MXU_DTYPE))
    if gates:
        in_specs.append(pl.BlockSpec((d, ML_GATE_PAD), lambda i, j: (0, 0)))
        args.append(gate_w)
    out_shape = [jax.ShapeDtypeStruct((m, n), ACT_DTYPE)]
    out_specs = [pl.BlockSpec((tm, tn), lambda i, j: (i, j))]
    if gates:
        out_shape += [jax.ShapeDtypeStruct((m, 2 * ML_HEADS), F32),
                      jax.ShapeDtypeStruct((2 * ML_HEADS, m), F32)]
        out_specs += [pl.BlockSpec((tm, 2 * ML_HEADS), lambda i, j: (i, 0)),
                      pl.BlockSpec((2 * ML_HEADS, tm), lambda i, j: (0, i))]
    res = pl.pallas_call(
        functools.partial(_proj_in_kernel, rope=rope, gates=gates, w_transposed=w_transposed),
        out_shape=out_shape, grid=grid, in_specs=in_specs, out_specs=out_specs,
        scratch_shapes=[pltpu.VMEM((tm, d), MXU_DTYPE)],
        compiler_params=_cparams(("parallel", "arbitrary")),
        name="proj_in",
    )(*args)
    return res if gates else res[0]


def _proj_out_kernel(*refs, final_norm):
    y_ref, w_ref, x_ref = refs[:3]
    if final_norm:
        g_ref, o_ref = refs[3:5]
    else:
        o_ref = refs[3]
    acc = x_ref[...] + jnp.dot(y_ref[...].astype(MXU_DTYPE), w_ref[...], preferred_element_type=F32)
    if final_norm:
        ms = jnp.mean(acc * acc, axis=1, keepdims=True)
        acc = acc * lax.rsqrt(ms + EPS) * g_ref[...]
    o_ref[...] = acc


def _proj_out(x, y, w, *, tm, final_g=None):
    m, d = x.shape
    k = y.shape[1]
    final_norm = final_g is not None
    in_specs = [pl.BlockSpec((tm, k), lambda i: (i, 0)),
                pl.BlockSpec((k, d), lambda i: (0, 0)),
                pl.BlockSpec((tm, d), lambda i: (i, 0))]
    args = [y, w, x]
    if final_norm:
        in_specs.append(pl.BlockSpec((1, d), lambda i: (0, 0)))
        args.append(final_g.reshape(1, d))
    return pl.pallas_call(
        functools.partial(_proj_out_kernel, final_norm=final_norm),
        out_shape=jax.ShapeDtypeStruct((m, d), F32), grid=(m // tm,),
        in_specs=in_specs, out_specs=pl.BlockSpec((tm, d), lambda i: (i, 0)),
        compiler_params=_cparams(("parallel",)),
        name="proj_out",
    )(*args)


def _mlstm_kernel(q_ref, k_ref, v_ref, o_ref, z_ref, gc_ref, gt_ref, bc_ref, br_ref, ng_ref,
                  c0_ref, n0_ref, m0_ref, y_ref, cs_ref, ns_ref, ms_ref, *, L):
    c = pl.program_id(1)

    @pl.when(c == 0)
    def _():
        cs_ref[...] = c0_ref[...]
        ns_ref[...] = n0_ref[...]
        ms_ref[...] = m0_ref[...]

    row = lax.broadcasted_iota(jnp.int32, (L, L), 0)
    col = lax.broadcasted_iota(jnp.int32, (L, L), 1)
    tril = col <= row
    scale = ML_DK ** -0.5

    gc = gc_ref[0] + bc_ref[...]
    lane16 = lax.broadcasted_iota(jnp.int32, gc.shape, 1)
    gc = jnp.where(lane16 < ML_HEADS, gc, _log_sigmoid(gc))
    gt = gt_ref[0, 0] + br_ref[...]
    sub16 = lax.broadcasted_iota(jnp.int32, gt.shape, 0)
    gt = jnp.where(sub16 < ML_HEADS, gt, _log_sigmoid(gt))

    for h in range(ML_HEADS):
        li_col = gc[:, h:h + 1]
        lf_col = gc[:, ML_HEADS + h:ML_HEADS + h + 1]
        li_row = gt[h:h + 1, :]
        lf_row = gt[ML_HEADS + h:ML_HEADS + h + 1, :]
        q = q_ref[0, :, h * ML_DK:(h + 1) * ML_DK]
        k = k_ref[0, :, h * ML_DK:(h + 1) * ML_DK]
        v = v_ref[0, :, h * ML_DV:(h + 1) * ML_DV]
        c_old = cs_ref[0, h]
        n_old = ns_ref[0, h:h + 1, :]
        m_old = ms_ref[0, h:h + 1, 0:1]

        b_col = jnp.sum(jnp.where(tril, lf_row, 0.0), axis=1, keepdims=True)
        b_row = jnp.sum(jnp.where(row <= col, lf_col, 0.0), axis=0, keepdims=True)
        dmat = jnp.where(tril, b_col - b_row + li_row, -jnp.inf)
        inter = b_col + m_old
        m_t = jnp.maximum(inter, jnp.max(dmat, axis=1, keepdims=True))
        p = jnp.exp(dmat - m_t)
        w_inter = jnp.exp(inter - m_t)
        s = p * (_nt_dot(q, k) * scale)
        qf = q.astype(F32)
        kf = k.astype(F32)
        num = w_inter * _nt_dot(q, c_old.astype(MXU_DTYPE)) + jnp.dot(
            s.astype(MXU_DTYPE), v, preferred_element_type=F32)
        den = w_inter * jnp.sum(qf * n_old, axis=1, keepdims=True) + jnp.sum(s, axis=1, keepdims=True)
        hh = num / jnp.maximum(jnp.abs(den), jnp.exp(-m_t))

        hn = hh * lax.rsqrt(jnp.mean(hh * hh, axis=1, keepdims=True) + EPS)
        hn = hn * ng_ref[:, h * ML_DV:(h + 1) * ML_DV]
        og = o_ref[0, :, h * ML_DV:(h + 1) * ML_DV].astype(F32)
        zg = z_ref[0, :, h * ML_DV:(h + 1) * ML_DV].astype(F32)
        y_ref[0, :, h * ML_DV:(h + 1) * ML_DV] = (hn * _sigmoid(og) * _silu(zg)).astype(y_ref.dtype)

        b_end = jnp.sum(lf_row, axis=1, keepdims=True)
        a_end = b_end - b_col + li_col
        m_new = jnp.maximum(b_end + m_old, jnp.max(a_end, axis=0, keepdims=True))
        w_old = jnp.exp(b_end + m_old - m_new)
        w_s = jnp.exp(a_end - m_new)
        vw = (v.astype(F32) * w_s).astype(MXU_DTYPE)
        cs_ref[0, h] = w_old * c_old + _tn_dot(vw, k) * scale
        ns_ref[0, h:h + 1, :] = w_old * n_old + jnp.sum(kf * w_s, axis=0, keepdims=True) * scale
        ms_ref[0, h:h + 1, :] = jnp.broadcast_to(m_new, (1, LANE))


def _mlstm_mixer(u, gc, gt, b_if, norm_g, c0, n0, m0, *, L):
    bsz, s, _ = u.shape
    nc = s // L
    shared = c0.shape[0] == 1
    st = (lambda b, c: (0, 0, 0, 0)) if shared else (lambda b, c: (b, 0, 0, 0))
    st3 = (lambda b, c: (0, 0, 0)) if shared else (lambda b, c: (b, 0, 0))
    qk_b = ML_QK_W
    v_b = ML_V_W
    in_specs = [
        pl.BlockSpec((1, L, qk_b), lambda b, c: (b, c, 0)),
        pl.BlockSpec((1, L, qk_b), lambda b, c: (b, c, 1)),
        pl.BlockSpec((1, L, v_b), lambda b, c: (b, c, 1)),
        pl.BlockSpec((1, L, v_b), lambda b, c: (b, c, 2)),
        pl.BlockSpec((1, L, v_b), lambda b, c: (b, c, 3)),
        pl.BlockSpec((1, L, 2 * ML_HEADS), lambda b, c: (b, c, 0)),
        pl.BlockSpec((1, 1, 2 * ML_HEADS, L), lambda b, c: (b, c, 0, 0)),
        pl.BlockSpec((1, 2 * ML_HEADS), lambda b, c: (0, 0)),
        pl.BlockSpec((2 * ML_HEADS, 1), lambda b, c: (0, 0)),
        pl.BlockSpec((1, ML_V_W), lambda b, c: (0, 0)),
        pl.BlockSpec((1, ML_HEADS, ML_DV, ML_DK), st),
        pl.BlockSpec((1, ML_HEADS, ML_DK), st3),
        pl.BlockSpec((1, ML_HEADS, LANE), st3),
    ]
    out_shape = [jax.ShapeDtypeStruct((bsz, s, ML_V_W), ACT_DTYPE),
                 jax.ShapeDtypeStruct((bsz, ML_HEADS, ML_DV, ML_DK), F32),
                 jax.ShapeDtypeStruct((bsz, ML_HEADS, ML_DK), F32),
                 jax.ShapeDtypeStruct((bsz, ML_HEADS, LANE), F32)]
    out_specs = [pl.BlockSpec((1, L, ML_V_W), lambda b, c: (b, c, 0)),
                 pl.BlockSpec((1, ML_HEADS, ML_DV, ML_DK), lambda b, c: (b, 0, 0, 0)),
                 pl.BlockSpec((1, ML_HEADS, ML_DK), lambda b, c: (b, 0, 0)),
                 pl.BlockSpec((1, ML_HEADS, LANE), lambda b, c: (b, 0, 0))]
    b_flat = b_if.reshape(2 * ML_HEADS).astype(F32)
    y, cN, nN, mN = pl.pallas_call(
        functools.partial(_mlstm_kernel, L=L),
        out_shape=out_shape, grid=(bsz, nc), in_specs=in_specs, out_specs=out_specs,
        compiler_params=_cparams(("parallel", "arbitrary")),
        name="mlstm_mixer",
    )(u, u, u, u, u, gc, gt, b_flat.reshape(1, -1), b_flat.reshape(-1, 1),
      norm_g.reshape(1, ML_V_W).astype(F32), c0, n0, m0)
    return y, cN, nN, mN[:, :, 0]


ML_ST_W = ML_DV + LANE
ML_EXT_ROWS = 16


def _split3(x):
    x1 = x.astype(jnp.bfloat16).astype(F32)
    r1 = x - x1
    x2 = r1.astype(jnp.bfloat16).astype(F32)
    x3 = (r1 - x2).astype(jnp.bfloat16).astype(F32)
    return x1, x2, x3


def _gate_scan_kernel(g_ref, br_ref, o_ref, *, L):
    H = ML_HEADS
    g = g_ref[...] + br_ref[...]
    li = g[0:H]
    pos = lax.rem(lax.broadcasted_iota(jnp.int32, li.shape, 1), L)
    b = _log_sigmoid(g[H:2 * H])
    d = 1
    while d < L:
        b = b + jnp.where(pos >= d, pltpu.roll(b, d, 1), 0.0)
        d *= 2
    beta0 = li - b
    cm = beta0
    d = 1
    while d < L:
        cm = jnp.maximum(cm, jnp.where(pos >= d, pltpu.roll(cm, d, 1), -jnp.inf))
        d *= 2
    o_ref[0:H, :] = beta0
    o_ref[H:2 * H, :] = b
    o_ref[2 * H:3 * H, :] = cm


def _gate_scan(gt, b_if, *, L, tb):
    m = gt.shape[1]
    return pl.pallas_call(
        functools.partial(_gate_scan_kernel, L=L),
        out_shape=jax.ShapeDtypeStruct((3 * ML_HEADS, m), F32), grid=(m // tb,),
        in_specs=[pl.BlockSpec((2 * ML_HEADS, tb), lambda i: (0, i)),
                  pl.BlockSpec((2 * ML_HEADS, 1), lambda i: (0, 0))],
        out_specs=pl.BlockSpec((3 * ML_HEADS, tb), lambda i: (0, i)),
        compiler_params=_cparams(("parallel",)),
        name="mlstm_gate_scan",
    )(gt, b_if.reshape(2 * ML_HEADS, 1).astype(F32))


def _mlstm_long_kernel(q_ref, k_ref, v_ref, o_ref, z_ref, gs_ref, ng_ref, c0_ref, n0_ref, m0_ref,
                       y_ref, cs_ref, ns_ref, ms_ref, st_sc, *, L):
    c = pl.program_id(1)
    nc = pl.num_programs(1)
    H = ML_HEADS
    bf = jnp.bfloat16

    @pl.when(c == 0)
    def _():
        for h in range(H):
            st_sc[h, :, 0:ML_DV] = c0_ref[0, h].T
            st_sc[h, :, ML_DV:ML_ST_W] = jnp.broadcast_to(n0_ref[0, h:h + 1, :], (ML_DK, LANE)).T
        ms_ref[...] = m0_ref[...]

    row = lax.broadcasted_iota(jnp.int32, (L, L), 0)
    col = lax.broadcasted_iota(jnp.int32, (L, L), 1)
    tril = col <= row

    beta0 = gs_ref[0:H, :]
    b = gs_ref[H:2 * H, :]
    cm = gs_ref[2 * H:3 * H, :]
    m_old = ms_ref[0]
    mm = jnp.maximum(jnp.concatenate([m_old] * (L // LANE), axis=1), cm)
    a3 = _split3(-mm)
    b3 = _split3(beta0 + jnp.log(jnp.float32(ML_DK ** -0.5)))
    n3 = _split3(-b)
    o3 = _split3(m_old)

    ones_l = jnp.ones((3, L), F32)
    ones_s = jnp.ones((3, LANE), F32)
    sub = lax.broadcasted_iota(jnp.int32, (ML_EXT_ROWS, LANE), 0)
    rhs_f = jnp.where((sub < 3) | ((sub >= 6) & (sub < 9)), 1.0, 0.0)
    ones_v = jnp.ones((L, LANE), MXU_DTYPE)
    mean_w = jnp.full((ML_DV, LANE), 1.0 / ML_DV, MXU_DTYPE)

    def front(h):
        hs = slice(h, h + 1)
        lhs = jnp.concatenate([a3[0][hs], a3[1][hs], a3[2][hs], ones_l, n3[0][hs], n3[1][hs], n3[2][hs],
                               jnp.zeros((ML_EXT_ROWS - 9, L), F32)], axis=0)
        rhs_e = jnp.concatenate([ones_l, b3[0][hs], b3[1][hs], b3[2][hs],
                                 jnp.zeros((ML_EXT_ROWS - 6, L), F32)], axis=0)
        rhs_g = jnp.concatenate([ones_s, o3[0][hs], o3[1][hs], o3[2][hs],
                                 jnp.zeros((ML_EXT_ROWS - 6, LANE), F32)], axis=0)
        rhs = jnp.concatenate([rhs_g, rhs_f, rhs_e], axis=1).astype(bf)
        ext = _tn_dot(lhs.astype(bf), rhs)
        ef = ext[:, LANE:2 * LANE]
        wg = jnp.exp(ext[:, 0:LANE])
        fl = jnp.exp(ef)
        p = jnp.exp(jnp.where(tril, ext[:, 2 * LANE:], -jnp.inf))

        q = q_ref[0, :, h * ML_DK:(h + 1) * ML_DK]
        k = k_ref[0, :, h * ML_DK:(h + 1) * ML_DK]
        v = v_ref[0, :, h * ML_DV:(h + 1) * ML_DV]
        s = p * _nt_dot(q, k)
        qw = q.astype(F32) * wg
        x = jnp.concatenate([qw.astype(MXU_DTYPE), s.astype(MXU_DTYPE)], axis=1)
        vo = jnp.concatenate([v.astype(MXU_DTYPE), ones_v], axis=1)
        st = st_sc[h]
        res = jnp.dot(x, jnp.concatenate([st.astype(MXU_DTYPE), vo], axis=0), preferred_element_type=F32)

        ktw = (k.astype(F32).T * p[L - 1:L, :]).astype(MXU_DTYPE)
        w_old = wg[L - 1:L, :]
        st_sc[h] = st * jnp.concatenate([w_old] * (ML_ST_W // LANE), axis=1) + jnp.dot(
            ktw, vo, preferred_element_type=F32)
        ms_ref[0, hs, :] = -ef[L - 1:L, :]
        return res, fl

    def back(h, res, fl):
        r = 1.0 / jnp.maximum(jnp.abs(res[:, ML_DV:]), fl)
        hh = res[:, :ML_DV] * jnp.concatenate([r, r], axis=1)
        msq = jnp.dot((hh * hh).astype(MXU_DTYPE), mean_w, preferred_element_type=F32)
        rs = lax.rsqrt(msq + EPS)
        hn = hh * jnp.concatenate([rs, rs], axis=1) * ng_ref[:, h * ML_DV:(h + 1) * ML_DV]
        og = o_ref[0, :, h * ML_DV:(h + 1) * ML_DV].astype(F32)
        zg = z_ref[0, :, h * ML_DV:(h + 1) * ML_DV].astype(F32)
        gate = (0.25 * zg) * (1.0 + jnp.tanh(0.5 * og)) * (1.0 + jnp.tanh(0.5 * zg))
        y_ref[0, :, h * ML_DV:(h + 1) * ML_DV] = (hn * gate).astype(y_ref.dtype)

    pending = None
    for h in range(H + 1):
        nxt = front(h) if h < H else None
        if pending is not None:
            back(h - 1, *pending)
        pending = nxt

    @pl.when(c == nc - 1)
    def _():
        for h in range(H):
            cs_ref[0, h] = st_sc[h, :, 0:ML_DV].T
            ns_ref[0, h:h + 1, :] = st_sc[h, :, ML_DV:ML_ST_W].T[0:1, :]


def _mlstm_mixer_long(u, gscan, norm_g, c0, n0, m0, *, L):
    bsz, s, _ = u.shape
    nc = s // L
    shared = c0.shape[0] == 1
    st = (lambda b, c: (0, 0, 0, 0)) if shared else (lambda b, c: (b, 0, 0, 0))
    st3 = (lambda b, c: (0, 0, 0)) if shared else (lambda b, c: (b, 0, 0))
    in_specs = [
        pl.BlockSpec((1, L, ML_QK_W), lambda b, c: (b, c, 0)),
        pl.BlockSpec((1, L, ML_QK_W), lambda b, c: (b, c, 1)),
        pl.BlockSpec((1, L, ML_V_W), lambda b, c: (b, c, 1)),
        pl.BlockSpec((1, L, ML_V_W), lambda b, c: (b, c, 2)),
        pl.BlockSpec((1, L, ML_V_W), lambda b, c: (b, c, 3)),
        pl.BlockSpec((3 * ML_HEADS, L), lambda b, c, nc=nc: (0, b * nc + c)),
        pl.BlockSpec((1, ML_V_W), lambda b, c: (0, 0)),
        pl.BlockSpec((1, ML_HEADS, ML_DV, ML_DK), st),
        pl.BlockSpec((1, ML_HEADS, ML_DK), st3),
        pl.BlockSpec((1, ML_HEADS, LANE), st3),
    ]
    out_shape = [jax.ShapeDtypeStruct((bsz, s, ML_V_W), ACT_DTYPE),
                 jax.ShapeDtypeStruct((bsz, ML_HEADS, ML_DV, ML_DK), F32),
                 jax.ShapeDtypeStruct((bsz, ML_HEADS, ML_DK), F32),
                 jax.ShapeDtypeStruct((bsz, ML_HEADS, LANE), F32)]
    out_specs = [pl.BlockSpec((1, L, ML_V_W), lambda b, c: (b, c, 0)),
                 pl.BlockSpec((1, ML_HEADS, ML_DV, ML_DK), lambda b, c: (b, 0, 0, 0)),
                 pl.BlockSpec((1, ML_HEADS, ML_DK), lambda b, c: (b, 0, 0)),
                 pl.BlockSpec((1, ML_HEADS, LANE), lambda b, c: (b, 0, 0))]
    y, cN, nN, mN = pl.pallas_call(
        functools.partial(_mlstm_long_kernel, L=L),
        out_shape=out_shape, grid=(bsz, nc), in_specs=in_specs, out_specs=out_specs,
        scratch_shapes=[pltpu.VMEM((ML_HEADS, ML_DK, ML_ST_W), F32)],
        compiler_params=_cparams(("parallel", "arbitrary")),
        name="mlstm_mixer_long",
    )(u, u, u, u, u, gscan, norm_g.reshape(1, ML_V_W).astype(F32), c0, n0, m0)
    return y, cN, nN, mN[:, :, 0]


LOG2E = 1.4426950408889634


def _swa_kernel(*refs, Lc, G, n_hist, n_invalid):
    sink_ref, q_ref, z_ref, kv_ref = refs[:4]
    pos = 4
    if n_hist:
        hist_ref = refs[pos]
        prev_refs = refs[pos + 1:pos + 1 + n_hist]
        pos += 1 + n_hist
    y_ref = refs[pos]
    c = pl.program_id(1)
    nk = (n_hist + 1) * Lc
    pairs = SW_HEADS // SW_KV // 2
    rows = pairs * Lc

    kv = kv_ref[0]
    if n_hist:
        prev = jnp.concatenate([r[0] for r in prev_refs], axis=0)
        kv = jnp.concatenate([jnp.where(c == 0, hist_ref[0], prev), kv], axis=0)
    kv = kv.astype(F32)
    nkr = kv.shape[0]
    lane = lax.broadcasted_iota(jnp.int32, (nkr, LANE), 1)
    low = lane < SW_HD
    blk = lax.broadcasted_iota(jnp.int32, (rows, 1), 0) // Lc
    ones_v = jnp.ones((nkr, LANE), MXU_DTYPE)

    def split(t, g):
        swapped = pltpu.roll(t, SW_HD, 1)
        if g % 2 == 0:
            lo, hi = t, swapped
        else:
            lo, hi = swapped, t
        return (jnp.where(low, lo, 0.0).astype(MXU_DTYPE), jnp.where(low, 0.0, hi).astype(MXU_DTYPE))

    for g in range(SW_KV):
        kt = kv[:, (g // 2) * LANE:(g // 2 + 1) * LANE] * (SW_HD ** -0.5 * LOG2E)
        vt = kv[:, SW_KV_W + (g // 2) * LANE:SW_KV_W + (g // 2 + 1) * LANE]
        k_par = split(kt, g)
        v_par = tuple(jnp.concatenate([vv, ones_v], axis=1) for vv in split(vt, g))
        sinks = []
        for par in range(2):
            sk = jnp.zeros((rows, 1), F32)
            for j in range(pairs):
                sk = jnp.where(blk == j, sink_ref[SW_HEADS // SW_KV * g + 2 * j + par] * LOG2E, sk)
            sinks.append(sk)

        def scores(i, k_par=k_par):
            ks = slice(i * Lc, i * Lc + nk)
            qs = jnp.concatenate([q_ref[0, i * Lc:(i + 1) * Lc, (pairs * g + j) * LANE:(pairs * g + j + 1) * LANE]
                                  for j in range(pairs)], axis=0)
            ss = []
            for par in range(2):
                s = _nt_dot(qs, k_par[par][ks])
                if n_invalid > i * Lc:
                    key_idx = (c * G + i) * Lc + lax.broadcasted_iota(jnp.int32, (1, nk), 1)
                    s = jnp.where(key_idx >= n_invalid, s, -jnp.inf)
                ss.append(s)
            return ss

        def attend(i, ss, g=g, v_par=v_par, sinks=sinks):
            ks = slice(i * Lc, i * Lc + nk)
            out = jnp.zeros((rows, LANE), F32)
            for par in range(2):
                mx = jnp.maximum(jnp.max(ss[par], axis=1, keepdims=True), sinks[par])
                p = jnp.exp2(ss[par] - mx)
                res = jnp.dot(p.astype(MXU_DTYPE), v_par[par][ks], preferred_element_type=F32)
                den = res[:, LANE:] + jnp.exp2(sinks[par] - mx)
                out = out + res[:, :LANE] / den
            for j in range(pairs):
                cs = slice((pairs * g + j) * LANE, (pairs * g + j + 1) * LANE)
                zg = z_ref[0, i * Lc:(i + 1) * Lc, cs].astype(F32)
                y_ref[0, i * Lc:(i + 1) * Lc, cs] = (out[j * Lc:(j + 1) * Lc] * _silu(zg)).astype(y_ref.dtype)

        pending = None
        for i in range(G + 1):
            nxt = scores(i) if i < G else None
            if pending is not None:
                attend(i - 1, pending)
            pending = nxt


def _swa_mixer(u, hist, sinks, *, Lc, G, n_invalid):
    bsz, s, _ = u.shape
    tq = G * Lc
    nc = s // tq
    kv_blk = 2 * SW_W // (2 * SW_KV_W)
    n_hist = 0 if hist is None else hist.shape[1] // Lc
    in_specs = [pl.BlockSpec(memory_space=pltpu.SMEM),
                pl.BlockSpec((1, tq, SW_W), lambda b, c: (b, c, 0)),
                pl.BlockSpec((1, tq, SW_W), lambda b, c: (b, c, 1)),
                pl.BlockSpec((1, tq, 2 * SW_KV_W), lambda b, c: (b, c, kv_blk))]
    args = [sinks.astype(F32), u, u, u]
    if n_hist:
        hmap = (lambda b, c: (0, 0, 0)) if hist.shape[0] == 1 else (lambda b, c: (b, 0, 0))
        in_specs.append(pl.BlockSpec((1, n_hist * Lc, 2 * SW_KV_W), hmap))
        args.append(hist)
        for i in range(n_hist):
            in_specs.append(pl.BlockSpec(
                (1, Lc, 2 * SW_KV_W),
                lambda b, c, i=i: (b, jnp.maximum(c * G - n_hist + i, 0), kv_blk)))
            args.append(u)
    return pl.pallas_call(
        functools.partial(_swa_kernel, Lc=Lc, G=G, n_hist=n_hist, n_invalid=n_invalid),
        out_shape=jax.ShapeDtypeStruct((bsz, s, SW_W), ACT_DTYPE), grid=(bsz, nc),
        in_specs=in_specs, out_specs=pl.BlockSpec((1, tq, SW_W), lambda b, c: (b, c, 0)),
        compiler_params=_cparams(("parallel", "parallel")),
        name="swa_mixer",
    )(*args)


def _pool_kernel(u_ref, z_ref, h0_ref, wg_ref, sc_ref, y_ref, e_sc, *, T, from_start):
    t = pl.program_id(1)

    @pl.when(t == 0)
    def _():
        e_sc[0:POOL_HIST, :] = h0_ref[0]

    u = u_ref[0].astype(F32)
    e_sc[POOL_HIST:POOL_HIST + T, :] = u
    for g, w in enumerate(POOL_WINDOWS):
        cs = slice(g * POOL_GC, (g + 1) * POOL_GC)
        ug = u[:, cs]
        acc = ug
        for d in range(1, w):
            acc = acc + e_sc[POOL_HIST - d:POOL_HIST - d + T, cs]
        if from_start:
            pos = t * T + lax.broadcasted_iota(jnp.int32, (T, 1), 0) + 1
            pooled = acc / jnp.minimum(pos, w).astype(F32)
        else:
            pooled = acc * (1.0 / w)
        pooled = pooled - ug
        mixed = jnp.dot(pooled.astype(MXU_DTYPE), wg_ref[g], preferred_element_type=F32)
        zg = z_ref[0, :, cs].astype(F32)
        y_ref[0, :, cs] = (mixed * sc_ref[:, cs] * _silu(zg)).astype(y_ref.dtype)
    e_sc[0:POOL_HIST, :] = e_sc[T:T + POOL_HIST, :]


def _pool_mixer(u, hist, w_grp, scale, *, T, from_start):
    bsz, s, w2 = u.shape
    w = w2 // 2
    nt = s // T
    hmap = (lambda b, t: (0, 0, 0)) if hist.shape[0] == 1 else (lambda b, t: (b, 0, 0))
    return pl.pallas_call(
        functools.partial(_pool_kernel, T=T, from_start=from_start),
        out_shape=jax.ShapeDtypeStruct((bsz, s, w), ACT_DTYPE), grid=(bsz, nt),
        in_specs=[pl.BlockSpec((1, T, w), lambda b, t: (b, t, 0)),
                  pl.BlockSpec((1, T, w), lambda b, t: (b, t, 1)),
                  pl.BlockSpec((1, POOL_HIST, w), hmap),
                  pl.BlockSpec((len(POOL_WINDOWS), POOL_GC, POOL_GC), lambda b, t: (0, 0, 0)),
                  pl.BlockSpec((1, w), lambda b, t: (0, 0))],
        out_specs=pl.BlockSpec((1, T, w), lambda b, t: (b, t, 0)),
        scratch_shapes=[pltpu.VMEM((POOL_HIST + T, w), F32)],
        compiler_params=_cparams(("parallel", "arbitrary")),
        name="pool_mixer",
    )(u, u, hist, w_grp, scale.reshape(1, w).astype(F32))


def _pool_band_matrices(T):
    t = np.arange(T)[:, None]
    k = np.arange(T)[None, :]
    th = np.arange(POOL_HIST)[:, None]
    ph = np.arange(POOL_HIST)[None, :] - POOL_HIST
    main = [((k <= t) & (k > t - w)) / w - (k == t) for w in POOL_WINDOWS]
    hist = [(ph > th - w) / w for w in POOL_WINDOWS]
    return np.stack(main).astype(np.float32), np.stack(hist).astype(np.float32)


def _pool_long_kernel(u_ref, z_ref, h0_ref, wg_ref, sc_ref, bm_ref, bh_ref, y_ref, hist_sc, *, T):
    t = pl.program_id(1)

    @pl.when(t == 0)
    def _():
        hist_sc[...] = h0_ref[0]

    for g in range(len(POOL_WINDOWS)):
        cs = slice(g * POOL_GC, (g + 1) * POOL_GC)
        pooled = jnp.dot(bm_ref[g], u_ref[0, :, cs].astype(MXU_DTYPE), preferred_element_type=F32)
        head = pooled[0:POOL_HIST] + jnp.dot(bh_ref[g], hist_sc[:, cs].astype(MXU_DTYPE),
                                             preferred_element_type=F32)
        pooled = jnp.concatenate([head, pooled[POOL_HIST:]], axis=0)
        mixed = jnp.dot(pooled.astype(MXU_DTYPE), wg_ref[g], preferred_element_type=F32)
        zg = z_ref[0, :, cs].astype(F32)
        y_ref[0, :, cs] = (mixed * sc_ref[:, cs] * _silu(zg)).astype(y_ref.dtype)
    hist_sc[...] = u_ref[0, T - POOL_HIST:T, 0:hist_sc.shape[1]]


def _pool_mixer_long(u, hist, w_grp, scale, *, T):
    assert jnp.dtype(ACT_DTYPE).itemsize <= jnp.dtype(MXU_DTYPE).itemsize
    bsz, s, w2 = u.shape
    w = w2 // 2
    nt = s // T
    ng = len(POOL_WINDOWS)
    bm, bh = _pool_band_matrices(T)
    hmap = (lambda b, t: (0, 0, 0)) if hist.shape[0] == 1 else (lambda b, t: (b, 0, 0))
    return pl.pallas_call(
        functools.partial(_pool_long_kernel, T=T),
        out_shape=jax.ShapeDtypeStruct((bsz, s, w), ACT_DTYPE), grid=(bsz, nt),
        in_specs=[pl.BlockSpec((1, T, w), lambda b, t: (b, t, 0)),
                  pl.BlockSpec((1, T, w), lambda b, t: (b, t, 1)),
                  pl.BlockSpec((1, POOL_HIST, w), hmap),
                  pl.BlockSpec((ng, POOL_GC, POOL_GC), lambda b, t: (0, 0, 0)),
                  pl.BlockSpec((1, w), lambda b, t: (0, 0)),
                  pl.BlockSpec((ng, T, T), lambda b, t: (0, 0, 0)),
                  pl.BlockSpec((ng, POOL_HIST, POOL_HIST), lambda b, t: (0, 0, 0))],
        out_specs=pl.BlockSpec((1, T, w), lambda b, t: (b, t, 0)),
        scratch_shapes=[pltpu.VMEM((POOL_HIST, w), ACT_DTYPE)],
        compiler_params=_cparams(("parallel", "arbitrary")),
        name="pool_mixer_long",
    )(u, u, hist, w_grp, scale.reshape(1, w).astype(F32),
      jnp.asarray(bm, MXU_DTYPE), jnp.asarray(bh, MXU_DTYPE))


def _rope_tables(pos):
    half = ROT_DIM // 2
    inv = jnp.power(ROPE_THETA, -jnp.arange(half, dtype=F32) / half)
    ang = pos.astype(F32)[:, None] * inv[None, :]
    cos, sin = jnp.cos(ang), jnp.sin(ang)
    n = pos.shape[0]
    pad = jnp.zeros((n, SW_HD - ROT_DIM), F32)
    ta = jnp.concatenate([cos, cos, pad + 1.0], axis=1)
    ts = jnp.concatenate([-sin, sin, pad], axis=1)
    return tuple(jnp.concatenate([t, t], axis=1) for t in (ta, ts))


def _rope_partner_matrix():
    half = ROT_DIM // 2
    p = np.zeros((LANE, LANE), np.float32)
    for d in range(LANE):
        if d % SW_HD < half:
            p[d + half, d] = 1.0
        elif d % SW_HD < ROT_DIM:
            p[d - half, d] = 1.0
    return p


def _pick_tile(m, pref):
    t = pref
    while m % t:
        t //= 2
    return t


def kernel(x_prompt, x_sample, state_mlstm_C, state_mlstm_n, state_mlstm_m, cache_swa_k, cache_swa_v, state_pool, meta_tokens, norm_g, final_norm_g, mlstm_w_in, mlstm_b_if, mlstm_norm_g, mlstm_w_out, swa_w_in, swa_sinks, swa_w_out, pool_w_in, pool_w_grp, pool_scale, pool_w_out):
    bp, sp, d = x_prompt.shape
    bs, ss, _ = x_sample.shape
    depth = norm_g.shape[0]
    mp = bp * sp
    ms_rows = bs * ss
    small = ms_rows + N_META

    xp = x_prompt.reshape(mp, d)
    xs = jnp.concatenate([x_sample.reshape(ms_rows, d), meta_tokens.astype(x_prompt.dtype)], axis=0)
    tm_p = _pick_tile(sp, 1024)
    tm_o = _pick_tile(sp, 512)
    ml_chunk = _pick_tile(sp, 256)
    pool_tile = _pick_tile(sp, 256)

    pC, pn, pm, pk, pv, pp = [], [], [], [], [], []
    sC, sn, sm, sk, sv, s_pool = [], [], [], [], [], []
    for i in range(depth):
        kind, j = i % N_MIXERS, i // N_MIXERS
        last = i == depth - 1
        fin = final_norm_g if last else None
        if kind == 0:
            w_in = mlstm_w_in[j]
            w_main = w_in.T[:ML_MAIN_W].astype(MXU_DTYPE)
            wg = w_in[:, ML_MAIN_W:]
            gate_w = jnp.pad(wg, ((0, 0), (0, ML_GATE_PAD - 2 * ML_HEADS))).astype(MXU_DTYPE)
            up, gcp, gtp = _proj_in(xp, norm_g[i], w_main, tm=tm_p, tn=1024, gate_w=gate_w, w_transposed=True)
            us, gcs, gts = _proj_in(xs, norm_g[i], w_main, tm=small, tn=1024, gate_w=gate_w, w_transposed=True)
            zc = jnp.zeros((1, ML_HEADS, ML_DV, ML_DK), F32)
            zn = jnp.zeros((1, ML_HEADS, ML_DK), F32)
            zm = jnp.zeros((1, ML_HEADS, LANE), F32)
            ym, c_m, n_m, m_m = _mlstm_mixer(
                us[ms_rows:].reshape(1, N_META, -1), gcs[ms_rows:].reshape(1, N_META, -1),
                gts[:, ms_rows:].reshape(1, 1, 2 * ML_HEADS, N_META),
                mlstm_b_if[j], mlstm_norm_g[j], zc, zn, zm, L=N_META)
            ysm, c_s, n_s, m_s = _mlstm_mixer(
                us[:ms_rows].reshape(bs, ss, -1), gcs[:ms_rows].reshape(bs, ss, -1),
                gts[:, :ms_rows].reshape(2 * ML_HEADS, bs, 1, ss).transpose(1, 2, 0, 3),
                mlstm_b_if[j], mlstm_norm_g[j], state_mlstm_C[j].astype(F32), state_mlstm_n[j].astype(F32),
                jnp.broadcast_to(state_mlstm_m[j].astype(F32)[..., None], (bs, ML_HEADS, LANE)), L=ss)
            yp, c_p, n_p, m_p = _mlstm_mixer_long(
                up.reshape(bp, sp, -1), _gate_scan(gtp, mlstm_b_if[j], L=ml_chunk, tb=sp),
                mlstm_norm_g[j], c_m, n_m,
                jnp.broadcast_to(m_m[..., None], (1, ML_HEADS, LANE)), L=ml_chunk)
            pC.append(c_p); pn.append(n_p); pm.append(m_p)
            sC.append(c_s); sn.append(n_s); sm.append(m_s)
            w_out = mlstm_w_out[j]
        elif kind == 1:
            w_in = swa_w_in[j]
            wq, wk, wv, wz = jnp.split(w_in, [SW_W, SW_W + SW_KV_W, SW_W + 2 * SW_KV_W], axis=1)
            w_perm = jnp.concatenate([wq, wz, wk, wv], axis=1).astype(MXU_DTYPE)
            n_in = w_perm.shape[1]
            rope_mask = np.concatenate([np.ones((1, SW_W)), np.zeros((1, SW_W)),
                                        np.ones((1, SW_KV_W)), np.zeros((1, SW_KV_W))], axis=1)
            tabs_p = _rope_tables(N_META + jnp.arange(sp))
            pos_s = jnp.concatenate([jnp.tile(N_META + PAST_LEN + jnp.arange(ss), bs), jnp.arange(N_META)])
            tabs_s = _rope_tables(pos_s)
            up = _proj_in(xp, norm_g[i], w_perm, tm=tm_p, tn=512, rope_tabs=tabs_p, rope_mask=rope_mask)
            us = _proj_in(xs, norm_g[i], w_perm, tm=small, tn=512, rope_tabs=tabs_s, rope_mask=rope_mask)
            up = up.reshape(bp, sp, n_in)
            kv_s = us[:ms_rows, 2 * SW_W:].reshape(bs, ss, 2 * SW_KV_W)
            kv_m = us[ms_rows:, 2 * SW_W:].reshape(1, N_META, 2 * SW_KV_W)
            ym = _swa_mixer(us[ms_rows:].reshape(1, N_META, n_in), None, swa_sinks[j],
                            Lc=N_META, G=1, n_invalid=0)
            cache = jnp.concatenate([cache_swa_k[j].reshape(bs, WINDOW, SW_KV_W),
                                     cache_swa_v[j].reshape(bs, WINDOW, SW_KV_W)], axis=-1).astype(ACT_DTYPE)
            ysm = _swa_mixer(us[:ms_rows].reshape(bs, ss, n_in), cache, swa_sinks[j],
                             Lc=SW_CHUNK, G=ss // SW_CHUNK, n_invalid=0)
            hist = jnp.concatenate([jnp.zeros((1, WINDOW - N_META, 2 * SW_KV_W), ACT_DTYPE), kv_m], axis=1)
            yp = _swa_mixer(up, hist, swa_sinks[j], Lc=SW_CHUNK, G=_pick_tile(sp // SW_CHUNK, 4),
                            n_invalid=WINDOW - N_META)
            kv_p = up[:, -WINDOW:, 2 * SW_W:]
            pk.append(kv_p[:, :, :SW_KV_W].astype(F32).reshape(bp, WINDOW, SW_KV, SW_HD))
            pv.append(kv_p[:, :, SW_KV_W:].astype(F32).reshape(bp, WINDOW, SW_KV, SW_HD))
            k_new = kv_s[:, :, :SW_KV_W].astype(F32).reshape(bs, ss, SW_KV, SW_HD)
            v_new = kv_s[:, :, SW_KV_W:].astype(F32).reshape(bs, ss, SW_KV, SW_HD)
            sk.append(jnp.concatenate([cache_swa_k[j].astype(F32), k_new], axis=1)[:, -WINDOW:])
            sv.append(jnp.concatenate([cache_swa_v[j].astype(F32), v_new], axis=1)[:, -WINDOW:])
            w_out = swa_w_out[j]
        else:
            w_in = pool_w_in[j].astype(MXU_DTYPE)
            pw = w_in.shape[1] // 2
            w_grp = pool_w_grp[j].astype(MXU_DTYPE)
            up = _proj_in(xp, norm_g[i], w_in, tm=tm_p, tn=1024).reshape(bp, sp, 2 * pw)
            us = _proj_in(xs, norm_g[i], w_in, tm=small, tn=1024)
            u_m = us[ms_rows:].reshape(1, N_META, 2 * pw)
            u_s = us[:ms_rows].reshape(bs, ss, 2 * pw)
            ym = _pool_mixer(u_m, jnp.zeros((1, POOL_HIST, pw), F32), w_grp, pool_scale[j],
                             T=N_META, from_start=True)
            hist_s = jnp.pad(state_pool[j].astype(F32), ((0, 0), (POOL_HIST - state_pool.shape[2], 0), (0, 0)))
            ysm = _pool_mixer(u_s, hist_s, w_grp, pool_scale[j], T=ss, from_start=False)
            yp = _pool_mixer_long(up, u_m[:, :, :pw], w_grp, pool_scale[j], T=pool_tile)
            n_keep = state_pool.shape[2]
            pp.append(up[:, -n_keep:, :pw].astype(F32))
            s_pool.append(u_s[:, -n_keep:, :pw].astype(F32))
            w_out = pool_w_out[j]
        w_out = w_out.astype(MXU_DTYPE)
        y_small = jnp.concatenate([ysm.reshape(ms_rows, -1), ym.reshape(N_META, -1)], axis=0)
        xp = _proj_out(xp, yp.reshape(mp, -1), w_out, tm=tm_o, final_g=fin)
        xs = _proj_out(xs, y_small, w_out, tm=small, final_g=fin)

    y_prompt = xp.reshape(bp, sp, d)
    y_sample = xs[:ms_rows].reshape(bs, ss, d)
    return (y_prompt, y_sample,
            jnp.stack(pC), jnp.stack(pn), jnp.stack(pm), jnp.stack(pk), jnp.stack(pv), jnp.stack(pp),
            jnp.stack(sC), jnp.stack(sn), jnp.stack(sm), jnp.stack(sk), jnp.stack(sv), jnp.stack(s_pool))
```

```python
import functools

import numpy as np
import jax
import jax.numpy as jnp
from jax import lax
from jax.experimental import pallas as pl
from jax.experimental.pallas import tpu as pltpu

F32 = jnp.float32
MXU_DTYPE = jnp.bfloat16
ACT_DTYPE = jnp.bfloat16

EPS = 1e-6
N_META = 16
N_MIXERS = 3
PAST_LEN = 1024

ML_HEADS = 8
ML_DK = 128
ML_DV = 256
ML_QK_W = ML_HEADS * ML_DK
ML_V_W = ML_HEADS * ML_DV
ML_MAIN_W = 2 * ML_QK_W + 3 * ML_V_W
ML_GATE_PAD = 128

SW_HEADS = 32
SW_KV = 4
SW_HD = 64
SW_W = SW_HEADS * SW_HD
SW_KV_W = SW_KV * SW_HD
WINDOW = 128
SW_CHUNK = 64
ROT_DIM = 16
ROPE_THETA = 500000.0

POOL_WINDOWS = (2, 4, 8, 16)
POOL_GC = 512
POOL_HIST = 16

LANE = 128
VMEM_LIMIT_BYTES = 56 * 1024 * 1024


def _cparams(sem):
    return pltpu.CompilerParams(dimension_semantics=sem, vmem_limit_bytes=VMEM_LIMIT_BYTES)


def _nt_dot(a, b):
    return lax.dot_general(a, b, (((1,), (1,)), ((), ())), preferred_element_type=F32)


def _tn_dot(a, b):
    return lax.dot_general(a, b, (((0,), (0,)), ((), ())), preferred_element_type=F32)


def _sigmoid(x):
    return 0.5 + 0.5 * jnp.tanh(0.5 * x)


def _silu(x):
    return x * _sigmoid(x)


def _log_sigmoid(x):
    return jnp.minimum(x, 0.0) - jnp.log1p(jnp.exp(-jnp.abs(x)))


def _proj_in_kernel(*refs, rope, gates, w_transposed):
    x_ref, g_ref, w_ref = refs[:3]
    pos = 3
    if rope:
        ta_ref, ts_ref, perm_ref = refs[pos:pos + 3]
        pos += 3
    if gates:
        wg_ref = refs[pos]
        pos += 1
    o_ref = refs[pos]
    pos += 1
    if gates:
        gc_ref, gt_ref = refs[pos:pos + 2]
        pos += 2
    hn_sc = refs[pos]
    j = pl.program_id(1)

    @pl.when(j == 0)
    def _():
        x = x_ref[...]
        ms = jnp.mean(x * x, axis=1, keepdims=True)
        hn = (x * lax.rsqrt(ms + EPS) * g_ref[...]).astype(MXU_DTYPE)
        hn_sc[...] = hn
        if gates:
            gcol = jnp.dot(hn, wg_ref[...], preferred_element_type=F32)
            gc_ref[...] = gcol[:, :2 * ML_HEADS]
            gt_ref[...] = gcol.T[:2 * ML_HEADS, :]

    if w_transposed:
        acc = _nt_dot(hn_sc[...], w_ref[...])
    else:
        acc = jnp.dot(hn_sc[...], w_ref[...], preferred_element_type=F32)
    if not rope:
        o_ref[...] = acc.astype(o_ref.dtype)
        return

    def store(pattern):
        pieces = []
        for l, rotate in enumerate(pattern):
            a = acc[:, l * LANE:(l + 1) * LANE]
            if rotate:
                partner = jnp.dot(a.astype(MXU_DTYPE), perm_ref[...], preferred_element_type=F32)
                a = a * ta_ref[...] + partner * ts_ref[...]
            pieces.append(a.astype(o_ref.dtype))
        o_ref[...] = jnp.concatenate(pieces, axis=1)

    for pattern in sorted(set(rope)):
        tiles = [t for t, p in enumerate(rope) if p == pattern]
        cond = functools.reduce(jnp.logical_or, [j == t for t in tiles])
        pl.when(cond)(functools.partial(store, pattern))


def _proj_in(x, g, w, *, tm, tn, n=None, rope_tabs=None, rope_mask=None, gate_w=None, w_transposed=False):
    m, d = x.shape
    if n is None:
        n = w.shape[0] if w_transposed else w.shape[1]
    rope = ()
    if rope_tabs is not None:
        groups = rope_mask.reshape(n // LANE, LANE)
        assert (groups == groups[:, :1]).all()
        rope = tuple(tuple(bool(f) for f in groups[t * tn // LANE:(t + 1) * tn // LANE, 0])
                     for t in range(n // tn))
    gates = gate_w is not None
    grid = (m // tm, n // tn)
    in_specs = [pl.BlockSpec((tm, d), lambda i, j: (i, 0)),
                pl.BlockSpec((1, d), lambda i, j: (0, 0)),
                pl.BlockSpec((tn, d), lambda i, j: (j, 0)) if w_transposed
                else pl.BlockSpec((d, tn), lambda i, j: (0, j))]
    args = [x, g.reshape(1, d), w]
    if rope:
        nrep = rope_tabs[0].shape[0] // tm
        for t in rope_tabs:
            in_specs.append(pl.BlockSpec((tm, LANE), lambda i, j, nrep=nrep: (i % nrep, 0)))
            args.append(t)
        in_specs.append(pl.BlockSpec((LANE, LANE), lambda i, j: (0, 0)))
        args.append(jnp.asarray(_rope_partner_matrix(), MXU_DTYPE))
    if gates:
        in_specs.append(pl.BlockSpec((d, ML_GATE_PAD), lambda i, j: (0, 0)))
        args.append(gate_w)
    out_shape = [jax.ShapeDtypeStruct((m, n), ACT_DTYPE)]
    out_specs = [pl.BlockSpec((tm, tn), lambda i, j: (i, j))]
    if gates:
        out_shape += [jax.ShapeDtypeStruct((m, 2 * ML_HEADS), F32),
                      jax.ShapeDtypeStruct((2 * ML_HEADS, m), F32)]
        out_specs += [pl.BlockSpec((tm, 2 * ML_HEADS), lambda i, j: (i, 0)),
                      pl.BlockSpec((2 * ML_HEADS, tm), lambda i, j: (0, i))]
    res = pl.pallas_call(
        functools.partial(_proj_in_kernel, rope=rope, gates=gates, w_transposed=w_transposed),
        out_shape=out_shape, grid=grid, in_specs=in_specs, out_specs=out_specs,
        scratch_shapes=[pltpu.VMEM((tm, d), MXU_DTYPE)],
        compiler_params=_cparams(("parallel", "arbitrary")),
        name="proj_in",
    )(*args)
    return res if gates else res[0]


def _proj_out_kernel(*refs, final_norm):
    y_ref, w_ref, x_ref = refs[:3]
    if final_norm:
        g_ref, o_ref = refs[3:5]
    else:
        o_ref = refs[3]
    acc = x_ref[...] + jnp.dot(y_ref[...].astype(MXU_DTYPE), w_ref[...], preferred_element_type=F32)
    if final_norm:
        ms = jnp.mean(acc * acc, axis=1, keepdims=True)
        acc = acc * lax.rsqrt(ms + EPS) * g_ref[...]
    o_ref[...] = acc


def _proj_out(x, y, w, *, tm, final_g=None):
    m, d = x.shape
    k = y.shape[1]
    final_norm = final_g is not None
    in_specs = [pl.BlockSpec((tm, k), lambda i: (i, 0)),
                pl.BlockSpec((k, d), lambda i: (0, 0)),
                pl.BlockSpec((tm, d), lambda i: (i, 0))]
    args = [y, w, x]
    if final_norm:
        in_specs.append(pl.BlockSpec((1, d), lambda i: (0, 0)))
        args.append(final_g.reshape(1, d))
    return pl.pallas_call(
        functools.partial(_proj_out_kernel, final_norm=final_norm),
        out_shape=jax.ShapeDtypeStruct((m, d), F32), grid=(m // tm,),
        in_specs=in_specs, out_specs=pl.BlockSpec((tm, d), lambda i: (i, 0)),
        compiler_params=_cparams(("parallel",)),
        name="proj_out",
    )(*args)


def _mlstm_kernel(q_ref, k_ref, v_ref, o_ref, z_ref, gc_ref, gt_ref, bc_ref, br_ref, ng_ref,
                  c0_ref, n0_ref, m0_ref, y_ref, cs_ref, ns_ref, ms_ref, *, L):
    c = pl.program_id(1)

    @pl.when(c == 0)
    def _():
        cs_ref[...] = c0_ref[...]
        ns_ref[...] = n0_ref[...]
        ms_ref[...] = m0_ref[...]

    row = lax.broadcasted_iota(jnp.int32, (L, L), 0)
    col = lax.broadcasted_iota(jnp.int32, (L, L), 1)
    tril = col <= row
    scale = ML_DK ** -0.5

    gc = gc_ref[0] + bc_ref[...]
    lane16 = lax.broadcasted_iota(jnp.int32, gc.shape, 1)
    gc = jnp.where(lane16 < ML_HEADS, gc, _log_sigmoid(gc))
    gt = gt_ref[0, 0] + br_ref[...]
    sub16 = lax.broadcasted_iota(jnp.int32, gt.shape, 0)
    gt = jnp.where(sub16 < ML_HEADS, gt, _log_sigmoid(gt))

    for h in range(ML_HEADS):
        li_col = gc[:, h:h + 1]
        lf_col = gc[:, ML_HEADS + h:ML_HEADS + h + 1]
        li_row = gt[h:h + 1, :]
        lf_row = gt[ML_HEADS + h:ML_HEADS + h + 1, :]
        q = q_ref[0, :, h * ML_DK:(h + 1) * ML_DK]
        k = k_ref[0, :, h * ML_DK:(h + 1) * ML_DK]
        v = v_ref[0, :, h * ML_DV:(h + 1) * ML_DV]
        c_old = cs_ref[0, h]
        n_old = ns_ref[0, h:h + 1, :]
        m_old = ms_ref[0, h:h + 1, 0:1]

        b_col = jnp.sum(jnp.where(tril, lf_row, 0.0), axis=1, keepdims=True)
        b_row = jnp.sum(jnp.where(row <= col, lf_col, 0.0), axis=0, keepdims=True)
        dmat = jnp.where(tril, b_col - b_row + li_row, -jnp.inf)
        inter = b_col + m_old
        m_t = jnp.maximum(inter, jnp.max(dmat, axis=1, keepdims=True))
        p = jnp.exp(dmat - m_t)
        w_inter = jnp.exp(inter - m_t)
        s = p * (_nt_dot(q, k) * scale)
        qf = q.astype(F32)
        kf = k.astype(F32)
        num = w_inter * _nt_dot(q, c_old.astype(MXU_DTYPE)) + jnp.dot(
            s.astype(MXU_DTYPE), v, preferred_element_type=F32)
        den = w_inter * jnp.sum(qf * n_old, axis=1, keepdims=True) + jnp.sum(s, axis=1, keepdims=True)
        hh = num / jnp.maximum(jnp.abs(den), jnp.exp(-m_t))

        hn = hh * lax.rsqrt(jnp.mean(hh * hh, axis=1, keepdims=True) + EPS)
        hn = hn * ng_ref[:, h * ML_DV:(h + 1) * ML_DV]
        og = o_ref[0, :, h * ML_DV:(h + 1) * ML_DV].astype(F32)
        zg = z_ref[0, :, h * ML_DV:(h + 1) * ML_DV].astype(F32)
        y_ref[0, :, h * ML_DV:(h + 1) * ML_DV] = (hn * _sigmoid(og) * _silu(zg)).astype(y_ref.dtype)

        b_end = jnp.sum(lf_row, axis=1, keepdims=True)
        a_end = b_end - b_col + li_col
        m_new = jnp.maximum(b_end + m_old, jnp.max(a_end, axis=0, keepdims=True))
        w_old = jnp.exp(b_end + m_old - m_new)
        w_s = jnp.exp(a_end - m_new)
        vw = (v.astype(F32) * w_s).astype(MXU_DTYPE)
        cs_ref[0, h] = w_old * c_old + _tn_dot(vw, k) * scale
        ns_ref[0, h:h + 1, :] = w_old * n_old + jnp.sum(kf * w_s, axis=0, keepdims=True) * scale
        ms_ref[0, h:h + 1, :] = jnp.broadcast_to(m_new, (1, LANE))


def _mlstm_mixer(u, gc, gt, b_if, norm_g, c0, n0, m0, *, L):
    bsz, s, _ = u.shape
    nc = s // L
    shared = c0.shape[0] == 1
    st = (lambda b, c: (0, 0, 0, 0)) if shared else (lambda b, c: (b, 0, 0, 0))
    st3 = (lambda b, c: (0, 0, 0)) if shared else (lambda b, c: (b, 0, 0))
    qk_b = ML_QK_W
    v_b = ML_V_W
    in_specs = [
        pl.BlockSpec((1, L, qk_b), lambda b, c: (b, c, 0)),
        pl.BlockSpec((1, L, qk_b), lambda b, c: (b, c, 1)),
        pl.BlockSpec((1, L, v_b), lambda b, c: (b, c, 1)),
        pl.BlockSpec((1, L, v_b), lambda b, c: (b, c, 2)),
        pl.BlockSpec((1, L, v_b), lambda b, c: (b, c, 3)),
        pl.BlockSpec((1, L, 2 * ML_HEADS), lambda b, c: (b, c, 0)),
        pl.BlockSpec((1, 1, 2 * ML_HEADS, L), lambda b, c: (b, c, 0, 0)),
        pl.BlockSpec((1, 2 * ML_HEADS), lambda b, c: (0, 0)),
        pl.BlockSpec((2 * ML_HEADS, 1), lambda b, c: (0, 0)),
        pl.BlockSpec((1, ML_V_W), lambda b, c: (0, 0)),
        pl.BlockSpec((1, ML_HEADS, ML_DV, ML_DK), st),
        pl.BlockSpec((1, ML_HEADS, ML_DK), st3),
        pl.BlockSpec((1, ML_HEADS, LANE), st3),
    ]
    out_shape = [jax.ShapeDtypeStruct((bsz, s, ML_V_W), ACT_DTYPE),
                 jax.ShapeDtypeStruct((bsz, ML_HEADS, ML_DV, ML_DK), F32),
                 jax.ShapeDtypeStruct((bsz, ML_HEADS, ML_DK), F32),
                 jax.ShapeDtypeStruct((bsz, ML_HEADS, LANE), F32)]
    out_specs = [pl.BlockSpec((1, L, ML_V_W), lambda b, c: (b, c, 0)),
                 pl.BlockSpec((1, ML_HEADS, ML_DV, ML_DK), lambda b, c: (b, 0, 0, 0)),
                 pl.BlockSpec((1, ML_HEADS, ML_DK), lambda b, c: (b, 0, 0)),
                 pl.BlockSpec((1, ML_HEADS, LANE), lambda b, c: (b, 0, 0))]
    b_flat = b_if.reshape(2 * ML_HEADS).astype(F32)
    y, cN, nN, mN = pl.pallas_call(
        functools.partial(_mlstm_kernel, L=L),
        out_shape=out_shape, grid=(bsz, nc), in_specs=in_specs, out_specs=out_specs,
        compiler_params=_cparams(("parallel", "arbitrary")),
        name="mlstm_mixer",
    )(u, u, u, u, u, gc, gt, b_flat.reshape(1, -1), b_flat.reshape(-1, 1),
      norm_g.reshape(1, ML_V_W).astype(F32), c0, n0, m0)
    return y, cN, nN, mN[:, :, 0]


ML_ST_W = ML_DV + LANE
ML_EXT_ROWS = 16


def _split3(x):
    x1 = x.astype(jnp.bfloat16).astype(F32)
    r1 = x - x1
    x2 = r1.astype(jnp.bfloat16).astype(F32)
    x3 = (r1 - x2).astype(jnp.bfloat16).astype(F32)
    return x1, x2, x3


def _gate_scan_kernel(g_ref, br_ref, o_ref, *, L):
    H = ML_HEADS
    g = g_ref[...] + br_ref[...]
    li = g[0:H]
    pos = lax.rem(lax.broadcasted_iota(jnp.int32, li.shape, 1), L)
    b = _log_sigmoid(g[H:2 * H])
    d = 1
    while d < L:
        b = b + jnp.where(pos >= d, pltpu.roll(b, d, 1), 0.0)
        d *= 2
    beta0 = li - b
    cm = beta0
    d = 1
    while d < L:
        cm = jnp.maximum(cm, jnp.where(pos >= d, pltpu.roll(cm, d, 1), -jnp.inf))
        d *= 2
    o_ref[0:H, :] = beta0
    o_ref[H:2 * H, :] = b
    o_ref[2 * H:3 * H, :] = cm


def _gate_scan(gt, b_if, *, L, tb):
    m = gt.shape[1]
    return pl.pallas_call(
        functools.partial(_gate_scan_kernel, L=L),
        out_shape=jax.ShapeDtypeStruct((3 * ML_HEADS, m), F32), grid=(m // tb,),
        in_specs=[pl.BlockSpec((2 * ML_HEADS, tb), lambda i: (0, i)),
                  pl.BlockSpec((2 * ML_HEADS, 1), lambda i: (0, 0))],
        out_specs=pl.BlockSpec((3 * ML_HEADS, tb), lambda i: (0, i)),
        compiler_params=_cparams(("parallel",)),
        name="mlstm_gate_scan",
    )(gt, b_if.reshape(2 * ML_HEADS, 1).astype(F32))


def _mlstm_long_kernel(q_ref, k_ref, v_ref, o_ref, z_ref, gs_ref, ng_ref, c0_ref, n0_ref, m0_ref,
                       y_ref, cs_ref, ns_ref, ms_ref, st_sc, *, L):
    c = pl.program_id(1)
    nc = pl.num_programs(1)
    H = ML_HEADS
    bf = jnp.bfloat16

    @pl.when(c == 0)
    def _():
        for h in range(H):
            st_sc[h, :, 0:ML_DV] = c0_ref[0, h].T
            st_sc[h, :, ML_DV:ML_ST_W] = jnp.broadcast_to(n0_ref[0, h:h + 1, :], (ML_DK, LANE)).T
        ms_ref[...] = m0_ref[...]

    row = lax.broadcasted_iota(jnp.int32, (L, L), 0)
    col = lax.broadcasted_iota(jnp.int32, (L, L), 1)
    tril = col <= row

    beta0 = gs_ref[0:H, :]
    b = gs_ref[H:2 * H, :]
    cm = gs_ref[2 * H:3 * H, :]
    m_old = ms_ref[0]
    mm = jnp.maximum(jnp.concatenate([m_old] * (L // LANE), axis=1), cm)
    a3 = _split3(-mm)
    b3 = _split3(beta0 + jnp.log(jnp.float32(ML_DK ** -0.5)))
    n3 = _split3(-b)
    o3 = _split3(m_old)

    ones_l = jnp.ones((3, L), F32)
    ones_s = jnp.ones((3, LANE), F32)
    sub = lax.broadcasted_iota(jnp.int32, (ML_EXT_ROWS, LANE), 0)
    rhs_f = jnp.where((sub < 3) | ((sub >= 6) & (sub < 9)), 1.0, 0.0)
    ones_v = jnp.ones((L, LANE), MXU_DTYPE)
    mean_w = jnp.full((ML_DV, LANE), 1.0 / ML_DV, MXU_DTYPE)

    def front(h):
        hs = slice(h, h + 1)
        lhs = jnp.concatenate([a3[0][hs], a3[1][hs], a3[2][hs], ones_l, n3[0][hs], n3[1][hs], n3[2][hs],
                               jnp.zeros((ML_EXT_ROWS - 9, L), F32)], axis=0)
        rhs_e = jnp.concatenate([ones_l, b3[0][hs], b3[1][hs], b3[2][hs],
                                 jnp.zeros((ML_EXT_ROWS - 6, L), F32)], axis=0)
        rhs_g = jnp.concatenate([ones_s, o3[0][hs], o3[1][hs], o3[2][hs],
                                 jnp.zeros((ML_EXT_ROWS - 6, LANE), F32)], axis=0)
        rhs = jnp.concatenate([rhs_g, rhs_f, rhs_e], axis=1).astype(bf)
        ext = _tn_dot(lhs.astype(bf), rhs)
        ef = ext[:, LANE:2 * LANE]
        wg = jnp.exp(ext[:, 0:LANE])
        fl = jnp.exp(ef)
        p = jnp.exp(jnp.where(tril, ext[:, 2 * LANE:], -jnp.inf))

        q = q_ref[0, :, h * ML_DK:(h + 1) * ML_DK]
        k = k_ref[0, :, h * ML_DK:(h + 1) * ML_DK]
        v = v_ref[0, :, h * ML_DV:(h + 1) * ML_DV]
        s = p * _nt_dot(q, k)
        qw = q.astype(F32) * wg
        x = jnp.concatenate([qw.astype(MXU_DTYPE), s.astype(MXU_DTYPE)], axis=1)
        vo = jnp.concatenate([v.astype(MXU_DTYPE), ones_v], axis=1)
        st = st_sc[h]
        res = jnp.dot(x, jnp.concatenate([st.astype(MXU_DTYPE), vo], axis=0), preferred_element_type=F32)

        ktw = (k.astype(F32).T * p[L - 1:L, :]).astype(MXU_DTYPE)
        w_old = wg[L - 1:L, :]
        st_sc[h] = st * jnp.concatenate([w_old] * (ML_ST_W // LANE), axis=1) + jnp.dot(
            ktw, vo, preferred_element_type=F32)
        ms_ref[0, hs, :] = -ef[L - 1:L, :]
        return res, fl

    def back(h, res, fl):
        r = 1.0 / jnp.maximum(jnp.abs(res[:, ML_DV:]), fl)
        hh = res[:, :ML_DV] * jnp.concatenate([r, r], axis=1)
        msq = jnp.dot((hh * hh).astype(MXU_DTYPE), mean_w, preferred_element_type=F32)
        rs = lax.rsqrt(msq + EPS)
        hn = hh * jnp.concatenate([rs, rs], axis=1) * ng_ref[:, h * ML_DV:(h + 1) * ML_DV]
        og = o_ref[0, :, h * ML_DV:(h + 1) * ML_DV].astype(F32)
        zg = z_ref[0, :, h * ML_DV:(h + 1) * ML_DV].astype(F32)
        gate = zg * (1.0 + jnp.tanh(0.5 * og)) * (1.0 + jnp.tanh(0.5 * zg))
        y_ref[0, :, h * ML_DV:(h + 1) * ML_DV] = (hn * gate).astype(y_ref.dtype)

    pending = None
    for h in range(H + 1):
        nxt = front(h) if h < H else None
        if pending is not None:
            back(h - 1, *pending)
        pending = nxt

    @pl.when(c == nc - 1)
    def _():
        for h in range(H):
            cs_ref[0, h] = st_sc[h, :, 0:ML_DV].T
            ns_ref[0, h:h + 1, :] = st_sc[h, :, ML_DV:ML_ST_W].T[0:1, :]


def _mlstm_mixer_long(u, gscan, norm_g, c0, n0, m0, *, L):
    bsz, s, _ = u.shape
    nc = s // L
    shared = c0.shape[0] == 1
    st = (lambda b, c: (0, 0, 0, 0)) if shared else (lambda b, c: (b, 0, 0, 0))
    st3 = (lambda b, c: (0, 0, 0)) if shared else (lambda b, c: (b, 0, 0))
    in_specs = [
        pl.BlockSpec((1, L, ML_QK_W), lambda b, c: (b, c, 0)),
        pl.BlockSpec((1, L, ML_QK_W), lambda b, c: (b, c, 1)),
        pl.BlockSpec((1, L, ML_V_W), lambda b, c: (b, c, 1)),
        pl.BlockSpec((1, L, ML_V_W), lambda b, c: (b, c, 2)),
        pl.BlockSpec((1, L, ML_V_W), lambda b, c: (b, c, 3)),
        pl.BlockSpec((3 * ML_HEADS, L), lambda b, c, nc=nc: (0, b * nc + c)),
        pl.BlockSpec((1, ML_V_W), lambda b, c: (0, 0)),
        pl.BlockSpec((1, ML_HEADS, ML_DV, ML_DK), st),
        pl.BlockSpec((1, ML_HEADS, ML_DK), st3),
        pl.BlockSpec((1, ML_HEADS, LANE), st3),
    ]
    out_shape = [jax.ShapeDtypeStruct((bsz, s, ML_V_W), ACT_DTYPE),
                 jax.ShapeDtypeStruct((bsz, ML_HEADS, ML_DV, ML_DK), F32),
                 jax.ShapeDtypeStruct((bsz, ML_HEADS, ML_DK), F32),
                 jax.ShapeDtypeStruct((bsz, ML_HEADS, LANE), F32)]
    out_specs = [pl.BlockSpec((1, L, ML_V_W), lambda b, c: (b, c, 0)),
                 pl.BlockSpec((1, ML_HEADS, ML_DV, ML_DK), lambda b, c: (b, 0, 0, 0)),
                 pl.BlockSpec((1, ML_HEADS, ML_DK), lambda b, c: (b, 0, 0)),
                 pl.BlockSpec((1, ML_HEADS, LANE), lambda b, c: (b, 0, 0))]
    y, cN, nN, mN = pl.pallas_call(
        functools.partial(_mlstm_long_kernel, L=L),
        out_shape=out_shape, grid=(bsz, nc), in_specs=in_specs, out_specs=out_specs,
        scratch_shapes=[pltpu.VMEM((ML_HEADS, ML_DK, ML_ST_W), F32)],
        compiler_params=_cparams(("parallel", "arbitrary")),
        name="mlstm_mixer_long",
    )(u, u, u, u, u, gscan, 0.25 * norm_g.reshape(1, ML_V_W).astype(F32), c0, n0, m0)
    return y, cN, nN, mN[:, :, 0]


LOG2E = 1.4426950408889634


def _swa_kernel(*refs, Lc, G, n_hist, n_invalid):
    sink_ref, q_ref, z_ref, kv_ref = refs[:4]
    pos = 4
    if n_hist:
        hist_ref = refs[pos]
        prev_refs = refs[pos + 1:pos + 1 + n_hist]
        pos += 1 + n_hist
    y_ref = refs[pos]
    c = pl.program_id(1)
    nk = (n_hist + 1) * Lc
    pairs = SW_HEADS // SW_KV // 2
    rows = pairs * Lc

    kv = kv_ref[0]
    if n_hist:
        prev = jnp.concatenate([r[0] for r in prev_refs], axis=0)
        kv = jnp.concatenate([jnp.where(c == 0, hist_ref[0], prev), kv], axis=0)
    kv = kv.astype(F32)
    nkr = kv.shape[0]
    lane = lax.broadcasted_iota(jnp.int32, (nkr, LANE), 1)
    low = lane < SW_HD
    blk = lax.broadcasted_iota(jnp.int32, (rows, 1), 0) // Lc
    ones_v = jnp.ones((nkr, LANE), MXU_DTYPE)

    def split(t, g):
        swapped = pltpu.roll(t, SW_HD, 1)
        if g % 2 == 0:
            lo, hi = t, swapped
        else:
            lo, hi = swapped, t
        return (jnp.where(low, lo, 0.0).astype(MXU_DTYPE), jnp.where(low, 0.0, hi).astype(MXU_DTYPE))

    for g in range(SW_KV):
        kt = kv[:, (g // 2) * LANE:(g // 2 + 1) * LANE] * (SW_HD ** -0.5 * LOG2E)
        vt = kv[:, SW_KV_W + (g // 2) * LANE:SW_KV_W + (g // 2 + 1) * LANE]
        k_par = split(kt, g)
        v_par = tuple(jnp.concatenate([vv, ones_v], axis=1) for vv in split(vt, g))
        sinks = []
        for par in range(2):
            sk = jnp.zeros((rows, 1), F32)
            for j in range(pairs):
                sk = jnp.where(blk == j, sink_ref[SW_HEADS // SW_KV * g + 2 * j + par] * LOG2E, sk)
            sinks.append(sk)

        def scores(i, k_par=k_par):
            ks = slice(i * Lc, i * Lc + nk)
            qs = jnp.concatenate([q_ref[0, i * Lc:(i + 1) * Lc, (pairs * g + j) * LANE:(pairs * g + j + 1) * LANE]
                                  for j in range(pairs)], axis=0)
            ss = []
            for par in range(2):
                s = _nt_dot(qs, k_par[par][ks])
                if n_invalid > i * Lc:
                    key_idx = (c * G + i) * Lc + lax.broadcasted_iota(jnp.int32, (1, nk), 1)
                    s = jnp.where(key_idx >= n_invalid, s, -jnp.inf)
                ss.append(s)
            return ss

        def attend(i, ss, g=g, v_par=v_par, sinks=sinks):
            ks = slice(i * Lc, i * Lc + nk)
            out = jnp.zeros((rows, LANE), F32)
            for par in range(2):
                mx = jnp.maximum(jnp.max(ss[par], axis=1, keepdims=True), sinks[par])
                p = jnp.exp2(ss[par] - mx)
                res = jnp.dot(p.astype(MXU_DTYPE), v_par[par][ks], preferred_element_type=F32)
                den = res[:, LANE:] + jnp.exp2(sinks[par] - mx)
                out = out + res[:, :LANE] / den
            for j in range(pairs):
                cs = slice((pairs * g + j) * LANE, (pairs * g + j + 1) * LANE)
                zg = z_ref[0, i * Lc:(i + 1) * Lc, cs].astype(F32)
                y_ref[0, i * Lc:(i + 1) * Lc, cs] = (out[j * Lc:(j + 1) * Lc] * _silu(zg)).astype(y_ref.dtype)

        pending = None
        for i in range(G + 1):
            nxt = scores(i) if i < G else None
            if pending is not None:
                attend(i - 1, pending)
            pending = nxt


def _swa_mixer(u, hist, sinks, *, Lc, G, n_invalid):
    bsz, s, _ = u.shape
    tq = G * Lc
    nc = s // tq
    kv_blk = 2 * SW_W // (2 * SW_KV_W)
    n_hist = 0 if hist is None else hist.shape[1] // Lc
    in_specs = [pl.BlockSpec(memory_space=pltpu.SMEM),
                pl.BlockSpec((1, tq, SW_W), lambda b, c: (b, c, 0)),
                pl.BlockSpec((1, tq, SW_W), lambda b, c: (b, c, 1)),
                pl.BlockSpec((1, tq, 2 * SW_KV_W), lambda b, c: (b, c, kv_blk))]
    args = [sinks.astype(F32), u, u, u]
    if n_hist:
        hmap = (lambda b, c: (0, 0, 0)) if hist.shape[0] == 1 else (lambda b, c: (b, 0, 0))
        in_specs.append(pl.BlockSpec((1, n_hist * Lc, 2 * SW_KV_W), hmap))
        args.append(hist)
        for i in range(n_hist):
            in_specs.append(pl.BlockSpec(
                (1, Lc, 2 * SW_KV_W),
                lambda b, c, i=i: (b, jnp.maximum(c * G - n_hist + i, 0), kv_blk)))
            args.append(u)
    return pl.pallas_call(
        functools.partial(_swa_kernel, Lc=Lc, G=G, n_hist=n_hist, n_invalid=n_invalid),
        out_shape=jax.ShapeDtypeStruct((bsz, s, SW_W), ACT_DTYPE), grid=(bsz, nc),
        in_specs=in_specs, out_specs=pl.BlockSpec((1, tq, SW_W), lambda b, c: (b, c, 0)),
        compiler_params=_cparams(("parallel", "parallel")),
        name="swa_mixer",
    )(*args)


def _pool_kernel(u_ref, z_ref, h0_ref, wg_ref, sc_ref, y_ref, e_sc, *, T, from_start):
    t = pl.program_id(1)

    @pl.when(t == 0)
    def _():
        e_sc[0:POOL_HIST, :] = h0_ref[0]

    u = u_ref[0].astype(F32)
    e_sc[POOL_HIST:POOL_HIST + T, :] = u
    for g, w in enumerate(POOL_WINDOWS):
        cs = slice(g * POOL_GC, (g + 1) * POOL_GC)
        ug = u[:, cs]
        acc = ug
        for d in range(1, w):
            acc = acc + e_sc[POOL_HIST - d:POOL_HIST - d + T, cs]
        if from_start:
            pos = t * T + lax.broadcasted_iota(jnp.int32, (T, 1), 0) + 1
            pooled = acc / jnp.minimum(pos, w).astype(F32)
        else:
            pooled = acc * (1.0 / w)
        pooled = pooled - ug
        mixed = jnp.dot(pooled.astype(MXU_DTYPE), wg_ref[g], preferred_element_type=F32)
        zg = z_ref[0, :, cs].astype(F32)
        y_ref[0, :, cs] = (mixed * sc_ref[:, cs] * _silu(zg)).astype(y_ref.dtype)
    e_sc[0:POOL_HIST, :] = e_sc[T:T + POOL_HIST, :]


def _pool_mixer(u, hist, w_grp, scale, *, T, from_start):
    bsz, s, w2 = u.shape
    w = w2 // 2
    nt = s // T
    hmap = (lambda b, t: (0, 0, 0)) if hist.shape[0] == 1 else (lambda b, t: (b, 0, 0))
    return pl.pallas_call(
        functools.partial(_pool_kernel, T=T, from_start=from_start),
        out_shape=jax.ShapeDtypeStruct((bsz, s, w), ACT_DTYPE), grid=(bsz, nt),
        in_specs=[pl.BlockSpec((1, T, w), lambda b, t: (b, t, 0)),
                  pl.BlockSpec((1, T, w), lambda b, t: (b, t, 1)),
                  pl.BlockSpec((1, POOL_HIST, w), hmap),
                  pl.BlockSpec((len(POOL_WINDOWS), POOL_GC, POOL_GC), lambda b, t: (0, 0, 0)),
                  pl.BlockSpec((1, w), lambda b, t: (0, 0))],
        out_specs=pl.BlockSpec((1, T, w), lambda b, t: (b, t, 0)),
        scratch_shapes=[pltpu.VMEM((POOL_HIST + T, w), F32)],
        compiler_params=_cparams(("parallel", "arbitrary")),
        name="pool_mixer",
    )(u, u, hist, w_grp, scale.reshape(1, w).astype(F32))


def _pool_band_matrices(T):
    t = np.arange(T)[:, None]
    k = np.arange(T)[None, :]
    th = np.arange(POOL_HIST)[:, None]
    ph = np.arange(POOL_HIST)[None, :] - POOL_HIST
    main = [((k <= t) & (k > t - w)) / w - (k == t) for w in POOL_WINDOWS]
    hist = [(ph > th - w) / w for w in POOL_WINDOWS]
    return np.stack(main).astype(np.float32), np.stack(hist).astype(np.float32)


def _pool_long_kernel(u_ref, z_ref, h0_ref, wg_ref, sc_ref, bm_ref, bh_ref, y_ref, hist_sc, *, T):
    t = pl.program_id(1)

    @pl.when(t == 0)
    def _():
        hist_sc[...] = h0_ref[0]

    for g in range(len(POOL_WINDOWS)):
        cs = slice(g * POOL_GC, (g + 1) * POOL_GC)
        pooled = jnp.dot(bm_ref[g], u_ref[0, :, cs].astype(MXU_DTYPE), preferred_element_type=F32)
        head = pooled[0:POOL_HIST] + jnp.dot(bh_ref[g], hist_sc[:, cs].astype(MXU_DTYPE),
                                             preferred_element_type=F32)
        pooled = jnp.concatenate([head, pooled[POOL_HIST:]], axis=0)
        mixed = jnp.dot(pooled.astype(MXU_DTYPE), wg_ref[g], preferred_element_type=F32)
        zg = z_ref[0, :, cs].astype(F32)
        y_ref[0, :, cs] = (mixed * sc_ref[:, cs] * _silu(zg)).astype(y_ref.dtype)
    hist_sc[...] = u_ref[0, T - POOL_HIST:T, 0:hist_sc.shape[1]]


def _pool_mixer_long(u, hist, w_grp, scale, *, T):
    assert jnp.dtype(ACT_DTYPE).itemsize <= jnp.dtype(MXU_DTYPE).itemsize
    bsz, s, w2 = u.shape
    w = w2 // 2
    nt = s // T
    ng = len(POOL_WINDOWS)
    bm, bh = _pool_band_matrices(T)
    hmap = (lambda b, t: (0, 0, 0)) if hist.shape[0] == 1 else (lambda b, t: (b, 0, 0))
    return pl.pallas_call(
        functools.partial(_pool_long_kernel, T=T),
        out_shape=jax.ShapeDtypeStruct((bsz, s, w), ACT_DTYPE), grid=(bsz, nt),
        in_specs=[pl.BlockSpec((1, T, w), lambda b, t: (b, t, 0)),
                  pl.BlockSpec((1, T, w), lambda b, t: (b, t, 1)),
                  pl.BlockSpec((1, POOL_HIST, w), hmap),
                  pl.BlockSpec((ng, POOL_GC, POOL_GC), lambda b, t: (0, 0, 0)),
                  pl.BlockSpec((1, w), lambda b, t: (0, 0)),
                  pl.BlockSpec((ng, T, T), lambda b, t: (0, 0, 0)),
                  pl.BlockSpec((ng, POOL_HIST, POOL_HIST), lambda b, t: (0, 0, 0))],
        out_specs=pl.BlockSpec((1, T, w), lambda b, t: (b, t, 0)),
        scratch_shapes=[pltpu.VMEM((POOL_HIST, w), ACT_DTYPE)],
        compiler_params=_cparams(("parallel", "arbitrary")),
        name="pool_mixer_long",
    )(u, u, hist, w_grp, scale.reshape(1, w).astype(F32),
      jnp.asarray(bm, MXU_DTYPE), jnp.asarray(bh, MXU_DTYPE))


def _rope_tables(pos):
    half = ROT_DIM // 2
    inv = np.power(ROPE_THETA, -np.arange(half, dtype=np.float64) / half)
    ang = np.asarray(pos, np.float64)[:, None] * inv[None, :]
    cos, sin = np.cos(ang), np.sin(ang)
    pad = np.zeros((ang.shape[0], SW_HD - ROT_DIM))
    ta = np.concatenate([cos, cos, pad + 1.0], axis=1)
    ts = np.concatenate([-sin, sin, pad], axis=1)
    return tuple(jnp.asarray(np.concatenate([t, t], axis=1), F32) for t in (ta, ts))


def _rope_partner_matrix():
    half = ROT_DIM // 2
    p = np.zeros((LANE, LANE), np.float32)
    for d in range(LANE):
        if d % SW_HD < half:
            p[d + half, d] = 1.0
        elif d % SW_HD < ROT_DIM:
            p[d - half, d] = 1.0
    return p


def _pick_tile(m, pref):
    t = pref
    while m % t:
        t //= 2
    return t


def kernel(x_prompt, x_sample, state_mlstm_C, state_mlstm_n, state_mlstm_m, cache_swa_k, cache_swa_v, state_pool, meta_tokens, norm_g, final_norm_g, mlstm_w_in, mlstm_b_if, mlstm_norm_g, mlstm_w_out, swa_w_in, swa_sinks, swa_w_out, pool_w_in, pool_w_grp, pool_scale, pool_w_out):
    bp, sp, d = x_prompt.shape
    bs, ss, _ = x_sample.shape
    depth = norm_g.shape[0]
    mp = bp * sp
    ms_rows = bs * ss
    small = ms_rows + N_META

    xp = x_prompt.reshape(mp, d)
    xs = jnp.concatenate([x_sample.reshape(ms_rows, d), meta_tokens.astype(x_prompt.dtype)], axis=0)
    tm_p = _pick_tile(sp, 1024)
    tm_o = _pick_tile(sp, 512)
    ml_chunk = _pick_tile(sp, 256)
    pool_tile = _pick_tile(sp, 256)

    pC, pn, pm, pk, pv, pp = [], [], [], [], [], []
    sC, sn, sm, sk, sv, s_pool = [], [], [], [], [], []
    for i in range(depth):
        kind, j = i % N_MIXERS, i // N_MIXERS
        last = i == depth - 1
        fin = final_norm_g if last else None
        if kind == 0:
            w_in = mlstm_w_in[j]
            w_t = w_in.T.astype(MXU_DTYPE)
            wg = w_in[:, ML_MAIN_W:]
            gate_w = jnp.pad(wg, ((0, 0), (0, ML_GATE_PAD - 2 * ML_HEADS))).astype(MXU_DTYPE)
            up, gcp, gtp = _proj_in(xp, norm_g[i], w_t, tm=tm_p, tn=1024, n=ML_MAIN_W, gate_w=gate_w,
                                    w_transposed=True)
            us, gcs, gts = _proj_in(xs, norm_g[i], w_t, tm=small, tn=1024, n=ML_MAIN_W, gate_w=gate_w,
                                    w_transposed=True)
            zc = jnp.zeros((1, ML_HEADS, ML_DV, ML_DK), F32)
            zn = jnp.zeros((1, ML_HEADS, ML_DK), F32)
            zm = jnp.zeros((1, ML_HEADS, LANE), F32)
            ym, c_m, n_m, m_m = _mlstm_mixer(
                us[ms_rows:].reshape(1, N_META, -1), gcs[ms_rows:].reshape(1, N_META, -1),
                gts[:, ms_rows:].reshape(1, 1, 2 * ML_HEADS, N_META),
                mlstm_b_if[j], mlstm_norm_g[j], zc, zn, zm, L=N_META)
            ysm, c_s, n_s, m_s = _mlstm_mixer(
                us[:ms_rows].reshape(bs, ss, -1), gcs[:ms_rows].reshape(bs, ss, -1),
                gts[:, :ms_rows].reshape(2 * ML_HEADS, bs, 1, ss).transpose(1, 2, 0, 3),
                mlstm_b_if[j], mlstm_norm_g[j], state_mlstm_C[j].astype(F32), state_mlstm_n[j].astype(F32),
                jnp.broadcast_to(state_mlstm_m[j].astype(F32)[..., None], (bs, ML_HEADS, LANE)), L=ss)
            yp, c_p, n_p, m_p = _mlstm_mixer_long(
                up.reshape(bp, sp, -1), _gate_scan(gtp, mlstm_b_if[j], L=ml_chunk, tb=sp),
                mlstm_norm_g[j], c_m, n_m,
                jnp.broadcast_to(m_m[..., None], (1, ML_HEADS, LANE)), L=ml_chunk)
            pC.append(c_p); pn.append(n_p); pm.append(m_p)
            sC.append(c_s); sn.append(n_s); sm.append(m_s)
            w_out = mlstm_w_out[j]
        elif kind == 1:
            w_in = swa_w_in[j]
            wq, wk, wv, wz = jnp.split(w_in, [SW_W, SW_W + SW_KV_W, SW_W + 2 * SW_KV_W], axis=1)
            w_perm = jnp.concatenate([wq, wz, wk, wv], axis=1).astype(MXU_DTYPE)
            n_in = w_perm.shape[1]
            rope_mask = np.concatenate([np.ones(SW_W, bool), np.zeros(SW_W, bool),
                                        np.ones(SW_KV_W, bool), np.zeros(SW_KV_W, bool)])
            tabs_p = _rope_tables(N_META + np.arange(sp))
            pos_s = np.concatenate([np.tile(N_META + PAST_LEN + np.arange(ss), bs), np.arange(N_META)])
            tabs_s = _rope_tables(pos_s)
            tn_sw = n_in // 3
            up = _proj_in(xp, norm_g[i], w_perm, tm=tm_p, tn=tn_sw, rope_tabs=tabs_p, rope_mask=rope_mask)
            us = _proj_in(xs, norm_g[i], w_perm, tm=small, tn=tn_sw, rope_tabs=tabs_s, rope_mask=rope_mask)
            up = up.reshape(bp, sp, n_in)
            kv_s = us[:ms_rows, 2 * SW_W:].reshape(bs, ss, 2 * SW_KV_W)
            kv_m = us[ms_rows:, 2 * SW_W:].reshape(1, N_META, 2 * SW_KV_W)
            ym = _swa_mixer(us[ms_rows:].reshape(1, N_META, n_in), None, swa_sinks[j],
                            Lc=N_META, G=1, n_invalid=0)
            cache = jnp.concatenate([cache_swa_k[j].reshape(bs, WINDOW, SW_KV_W),
                                     cache_swa_v[j].reshape(bs, WINDOW, SW_KV_W)], axis=-1).astype(ACT_DTYPE)
            ysm = _swa_mixer(us[:ms_rows].reshape(bs, ss, n_in), cache, swa_sinks[j],
                             Lc=SW_CHUNK, G=ss // SW_CHUNK, n_invalid=0)
            hist = jnp.concatenate([jnp.zeros((1, WINDOW - N_META, 2 * SW_KV_W), ACT_DTYPE), kv_m], axis=1)
            yp = _swa_mixer(up, hist, swa_sinks[j], Lc=SW_CHUNK, G=_pick_tile(sp // SW_CHUNK, 8),
                            n_invalid=WINDOW - N_META)
            kv_p = up[:, -WINDOW:, 2 * SW_W:]
            pk.append(kv_p[:, :, :SW_KV_W].astype(F32).reshape(bp, WINDOW, SW_KV, SW_HD))
            pv.append(kv_p[:, :, SW_KV_W:].astype(F32).reshape(bp, WINDOW, SW_KV, SW_HD))
            k_new = kv_s[:, :, :SW_KV_W].astype(F32).reshape(bs, ss, SW_KV, SW_HD)
            v_new = kv_s[:, :, SW_KV_W:].astype(F32).reshape(bs, ss, SW_KV, SW_HD)
            sk.append(jnp.concatenate([cache_swa_k[j].astype(F32), k_new], axis=1)[:, -WINDOW:])
            sv.append(jnp.concatenate([cache_swa_v[j].astype(F32), v_new], axis=1)[:, -WINDOW:])
            w_out = swa_w_out[j]
        else:
            w_in = pool_w_in[j].astype(MXU_DTYPE)
            pw = w_in.shape[1] // 2
            w_grp = pool_w_grp[j].astype(MXU_DTYPE)
            up = _proj_in(xp, norm_g[i], w_in, tm=tm_p, tn=1024).reshape(bp, sp, 2 * pw)
            us = _proj_in(xs, norm_g[i], w_in, tm=small, tn=1024)
            u_m = us[ms_rows:].reshape(1, N_META, 2 * pw)
            u_s = us[:ms_rows].reshape(bs, ss, 2 * pw)
            ym = _pool_mixer(u_m, jnp.zeros((1, POOL_HIST, pw), F32), w_grp, pool_scale[j],
                             T=N_META, from_start=True)
            hist_s = jnp.pad(state_pool[j].astype(F32), ((0, 0), (POOL_HIST - state_pool.shape[2], 0), (0, 0)))
            ysm = _pool_mixer(u_s, hist_s, w_grp, pool_scale[j], T=ss, from_start=False)
            yp = _pool_mixer_long(up, u_m[:, :, :pw], w_grp, pool_scale[j], T=pool_tile)
            n_keep = state_pool.shape[2]
            pp.append(up[:, -n_keep:, :pw].astype(F32))
            s_pool.append(u_s[:, -n_keep:, :pw].astype(F32))
            w_out = pool_w_out[j]
        w_out = w_out.astype(MXU_DTYPE)
        y_small = jnp.concatenate([ysm.reshape(ms_rows, -1), ym.reshape(N_META, -1)], axis=0)
        xp = _proj_out(xp, yp.reshape(mp, -1), w_out, tm=tm_o, final_g=fin)
        xs = _proj_out(xs, y_small, w_out, tm=small, final_g=fin)

    y_prompt = xp.reshape(bp, sp, d)
    y_sample = xs[:ms_rows].reshape(bs, ss, d)
    return (y_prompt, y_sample,
            jnp.stack(pC), jnp.stack(pn), jnp.stack(pm), jnp.stack(pk), jnp.stack(pv), jnp.stack(pp),
            jnp.stack(sC), jnp.stack(sn), jnp.stack(sm), jnp.stack(sk), jnp.stack(sv), jnp.stack(s_pool))
```

```python
import functools

import numpy as np
import jax
import jax.numpy as jnp
from jax import lax
from jax.experimental import pallas as pl
from jax.experimental.pallas import tpu as pltpu

F32 = jnp.float32
MXU_DTYPE = jnp.bfloat16
ACT_DTYPE = jnp.bfloat16

EPS = 1e-6
N_META = 16
N_MIXERS = 3
PAST_LEN = 1024

ML_HEADS = 8
ML_DK = 128
ML_DV = 256
ML_QK_W = ML_HEADS * ML_DK
ML_V_W = ML_HEADS * ML_DV
ML_MAIN_W = 2 * ML_QK_W + 3 * ML_V_W
ML_GATE_PAD = 128

SW_HEADS = 32
SW_KV = 4
SW_HD = 64
SW_W = SW_HEADS * SW_HD
SW_KV_W = SW_KV * SW_HD
WINDOW = 128
SW_CHUNK = 64
ROT_DIM = 16
ROPE_THETA = 500000.0

POOL_WINDOWS = (2, 4, 8, 16)
POOL_GC = 512
POOL_HIST = 16

LANE = 128
VMEM_LIMIT_BYTES = 56 * 1024 * 1024


def _cparams(sem):
    return pltpu.CompilerParams(dimension_semantics=sem, vmem_limit_bytes=VMEM_LIMIT_BYTES)


def _nt_dot(a, b):
    return lax.dot_general(a, b, (((1,), (1,)), ((), ())), preferred_element_type=F32)


def _tn_dot(a, b):
    return lax.dot_general(a, b, (((0,), (0,)), ((), ())), preferred_element_type=F32)


def _sigmoid(x):
    return 0.5 + 0.5 * jnp.tanh(0.5 * x)


def _silu(x):
    return x * _sigmoid(x)


def _log_sigmoid(x):
    return jnp.minimum(x, 0.0) - jnp.log1p(jnp.exp(-jnp.abs(x)))


def _proj_in_kernel(*refs, rope, gates, w_transposed):
    x_ref, g_ref, w_ref = refs[:3]
    pos = 3
    if rope:
        ta_ref, ts_ref, perm_ref = refs[pos:pos + 3]
        pos += 3
    if gates:
        wg_ref = refs[pos]
        pos += 1
    o_ref = refs[pos]
    pos += 1
    if gates:
        gc_ref, gt_ref = refs[pos:pos + 2]
        pos += 2
    hn_sc = refs[pos]
    j = pl.program_id(1)

    @pl.when(j == 0)
    def _():
        x = x_ref[...]
        ms = jnp.mean(x * x, axis=1, keepdims=True)
        hn = (x * lax.rsqrt(ms + EPS) * g_ref[...]).astype(MXU_DTYPE)
        hn_sc[...] = hn
        if gates:
            gcol = _nt_dot(hn, wg_ref[...])
            gc_ref[...] = gcol[:, :2 * ML_HEADS]
            gt_ref[...] = gcol.T[:2 * ML_HEADS, :]

    if w_transposed:
        acc = _nt_dot(hn_sc[...], w_ref[...])
    else:
        acc = jnp.dot(hn_sc[...], w_ref[...], preferred_element_type=F32)
    if not rope:
        o_ref[...] = acc.astype(o_ref.dtype)
        return

    def store(pattern):
        pieces = []
        for l, rotate in enumerate(pattern):
            a = acc[:, l * LANE:(l + 1) * LANE]
            if rotate:
                partner = jnp.dot(a.astype(MXU_DTYPE), perm_ref[...], preferred_element_type=F32)
                a = a * ta_ref[...] + partner * ts_ref[...]
            pieces.append(a.astype(o_ref.dtype))
        o_ref[...] = jnp.concatenate(pieces, axis=1)

    for pattern in sorted(set(rope)):
        tiles = [t for t, p in enumerate(rope) if p == pattern]
        cond = functools.reduce(jnp.logical_or, [j == t for t in tiles])
        pl.when(cond)(functools.partial(store, pattern))


def _proj_in(x, g, w, *, tm, tn, n=None, rope_tabs=None, rope_mask=None, gate_w=None, w_transposed=False):
    m, d = x.shape
    if n is None:
        n = w.shape[0] if w_transposed else w.shape[1]
    rope = ()
    if rope_tabs is not None:
        groups = rope_mask.reshape(n // LANE, LANE)
        assert (groups == groups[:, :1]).all()
        rope = tuple(tuple(bool(f) for f in groups[t * tn // LANE:(t + 1) * tn // LANE, 0])
                     for t in range(n // tn))
    gates = gate_w is not None
    grid = (m // tm, n // tn)
    in_specs = [pl.BlockSpec((tm, d), lambda i, j: (i, 0)),
                pl.BlockSpec((1, d), lambda i, j: (0, 0)),
                pl.BlockSpec((tn, d), lambda i, j: (j, 0)) if w_transposed
                else pl.BlockSpec((d, tn), lambda i, j: (0, j))]
    args = [x, g.reshape(1, d), w]
    if rope:
        nrep = rope_tabs[0].shape[0] // tm
        for t in rope_tabs:
            in_specs.append(pl.BlockSpec((tm, LANE), lambda i, j, nrep=nrep: (i % nrep, 0)))
            args.append(t)
        in_specs.append(pl.BlockSpec((LANE, LANE), lambda i, j: (0, 0)))
        args.append(jnp.asarray(_rope_partner_matrix(), MXU_DTYPE))
    if gates:
        in_specs.append(pl.BlockSpec((ML_GATE_PAD, d), lambda i, j: (0, 0)))
        args.append(gate_w)
    out_shape = [jax.ShapeDtypeStruct((m, n), ACT_DTYPE)]
    out_specs = [pl.BlockSpec((tm, tn), lambda i, j: (i, j))]
    if gates:
        out_shape += [jax.ShapeDtypeStruct((m, 2 * ML_HEADS), F32),
                      jax.ShapeDtypeStruct((2 * ML_HEADS, m), F32)]
        out_specs += [pl.BlockSpec((tm, 2 * ML_HEADS), lambda i, j: (i, 0)),
                      pl.BlockSpec((2 * ML_HEADS, tm), lambda i, j: (0, i))]
    res = pl.pallas_call(
        functools.partial(_proj_in_kernel, rope=rope, gates=gates, w_transposed=w_transposed),
        out_shape=out_shape, grid=grid, in_specs=in_specs, out_specs=out_specs,
        scratch_shapes=[pltpu.VMEM((tm, d), MXU_DTYPE)],
        compiler_params=_cparams(("parallel", "arbitrary")),
        name="proj_in",
    )(*args)
    return res if gates else res[0]


def _proj_out_kernel(*refs, final_norm):
    y_ref, w_ref, x_ref = refs[:3]
    if final_norm:
        g_ref, o_ref = refs[3:5]
    else:
        o_ref = refs[3]
    acc = x_ref[...] + jnp.dot(y_ref[...].astype(MXU_DTYPE), w_ref[...], preferred_element_type=F32)
    if final_norm:
        ms = jnp.mean(acc * acc, axis=1, keepdims=True)
        acc = acc * lax.rsqrt(ms + EPS) * g_ref[...]
    o_ref[...] = acc


def _proj_out(x, y, w, *, tm, final_g=None):
    m, d = x.shape
    k = y.shape[1]
    final_norm = final_g is not None
    in_specs = [pl.BlockSpec((tm, k), lambda i: (i, 0)),
                pl.BlockSpec((k, d), lambda i: (0, 0)),
                pl.BlockSpec((tm, d), lambda i: (i, 0))]
    args = [y, w, x]
    if final_norm:
        in_specs.append(pl.BlockSpec((1, d), lambda i: (0, 0)))
        args.append(final_g.reshape(1, d))
    return pl.pallas_call(
        functools.partial(_proj_out_kernel, final_norm=final_norm),
        out_shape=jax.ShapeDtypeStruct((m, d), F32), grid=(m // tm,),
        in_specs=in_specs, out_specs=pl.BlockSpec((tm, d), lambda i: (i, 0)),
        compiler_params=_cparams(("parallel",)),
        name="proj_out",
    )(*args)


def _mlstm_kernel(q_ref, k_ref, v_ref, o_ref, z_ref, gc_ref, gt_ref, bc_ref, br_ref, ng_ref,
                  c0_ref, n0_ref, m0_ref, y_ref, cs_ref, ns_ref, ms_ref, *, L):
    c = pl.program_id(1)

    @pl.when(c == 0)
    def _():
        cs_ref[...] = c0_ref[...]
        ns_ref[...] = n0_ref[...]
        ms_ref[...] = m0_ref[...]

    row = lax.broadcasted_iota(jnp.int32, (L, L), 0)
    col = lax.broadcasted_iota(jnp.int32, (L, L), 1)
    tril = col <= row
    scale = ML_DK ** -0.5

    gc = gc_ref[0] + bc_ref[...]
    lane16 = lax.broadcasted_iota(jnp.int32, gc.shape, 1)
    gc = jnp.where(lane16 < ML_HEADS, gc, _log_sigmoid(gc))
    gt = gt_ref[0, 0] + br_ref[...]
    sub16 = lax.broadcasted_iota(jnp.int32, gt.shape, 0)
    gt = jnp.where(sub16 < ML_HEADS, gt, _log_sigmoid(gt))

    for h in range(ML_HEADS):
        li_col = gc[:, h:h + 1]
        lf_col = gc[:, ML_HEADS + h:ML_HEADS + h + 1]
        li_row = gt[h:h + 1, :]
        lf_row = gt[ML_HEADS + h:ML_HEADS + h + 1, :]
        q = q_ref[0, :, h * ML_DK:(h + 1) * ML_DK]
        k = k_ref[0, :, h * ML_DK:(h + 1) * ML_DK]
        v = v_ref[0, :, h * ML_DV:(h + 1) * ML_DV]
        c_old = cs_ref[0, h]
        n_old = ns_ref[0, h:h + 1, :]
        m_old = ms_ref[0, h:h + 1, 0:1]

        b_col = jnp.sum(jnp.where(tril, lf_row, 0.0), axis=1, keepdims=True)
        b_row = jnp.sum(jnp.where(row <= col, lf_col, 0.0), axis=0, keepdims=True)
        dmat = jnp.where(tril, b_col - b_row + li_row, -jnp.inf)
        inter = b_col + m_old
        m_t = jnp.maximum(inter, jnp.max(dmat, axis=1, keepdims=True))
        p = jnp.exp(dmat - m_t)
        w_inter = jnp.exp(inter - m_t)
        s = p * (_nt_dot(q, k) * scale)
        qf = q.astype(F32)
        kf = k.astype(F32)
        num = w_inter * _nt_dot(q, c_old.astype(MXU_DTYPE)) + jnp.dot(
            s.astype(MXU_DTYPE), v, preferred_element_type=F32)
        den = w_inter * jnp.sum(qf * n_old, axis=1, keepdims=True) + jnp.sum(s, axis=1, keepdims=True)
        hh = num / jnp.maximum(jnp.abs(den), jnp.exp(-m_t))

        hn = hh * lax.rsqrt(jnp.mean(hh * hh, axis=1, keepdims=True) + EPS)
        hn = hn * ng_ref[:, h * ML_DV:(h + 1) * ML_DV]
        og = o_ref[0, :, h * ML_DV:(h + 1) * ML_DV].astype(F32)
        zg = z_ref[0, :, h * ML_DV:(h + 1) * ML_DV].astype(F32)
        y_ref[0, :, h * ML_DV:(h + 1) * ML_DV] = (hn * _sigmoid(og) * _silu(zg)).astype(y_ref.dtype)

        b_end = jnp.sum(lf_row, axis=1, keepdims=True)
        a_end = b_end - b_col + li_col
        m_new = jnp.maximum(b_end + m_old, jnp.max(a_end, axis=0, keepdims=True))
        w_old = jnp.exp(b_end + m_old - m_new)
        w_s = jnp.exp(a_end - m_new)
        vw = (v.astype(F32) * w_s).astype(MXU_DTYPE)
        cs_ref[0, h] = w_old * c_old + _tn_dot(vw, k) * scale
        ns_ref[0, h:h + 1, :] = w_old * n_old + jnp.sum(kf * w_s, axis=0, keepdims=True) * scale
        ms_ref[0, h:h + 1, :] = jnp.broadcast_to(m_new, (1, LANE))


def _mlstm_mixer(u, gc, gt, b_if, norm_g, c0, n0, m0, *, L):
    bsz, s, _ = u.shape
    nc = s // L
    shared = c0.shape[0] == 1
    st = (lambda b, c: (0, 0, 0, 0)) if shared else (lambda b, c: (b, 0, 0, 0))
    st3 = (lambda b, c: (0, 0, 0)) if shared else (lambda b, c: (b, 0, 0))
    qk_b = ML_QK_W
    v_b = ML_V_W
    in_specs = [
        pl.BlockSpec((1, L, qk_b), lambda b, c: (b, c, 0)),
        pl.BlockSpec((1, L, qk_b), lambda b, c: (b, c, 1)),
        pl.BlockSpec((1, L, v_b), lambda b, c: (b, c, 1)),
        pl.BlockSpec((1, L, v_b), lambda b, c: (b, c, 2)),
        pl.BlockSpec((1, L, v_b), lambda b, c: (b, c, 3)),
        pl.BlockSpec((1, L, 2 * ML_HEADS), lambda b, c: (b, c, 0)),
        pl.BlockSpec((1, 1, 2 * ML_HEADS, L), lambda b, c: (b, c, 0, 0)),
        pl.BlockSpec((1, 2 * ML_HEADS), lambda b, c: (0, 0)),
        pl.BlockSpec((2 * ML_HEADS, 1), lambda b, c: (0, 0)),
        pl.BlockSpec((1, ML_V_W), lambda b, c: (0, 0)),
        pl.BlockSpec((1, ML_HEADS, ML_DV, ML_DK), st),
        pl.BlockSpec((1, ML_HEADS, ML_DK), st3),
        pl.BlockSpec((1, ML_HEADS, LANE), st3),
    ]
    out_shape = [jax.ShapeDtypeStruct((bsz, s, ML_V_W), ACT_DTYPE),
                 jax.ShapeDtypeStruct((bsz, ML_HEADS, ML_DV, ML_DK), F32),
                 jax.ShapeDtypeStruct((bsz, ML_HEADS, ML_DK), F32),
                 jax.ShapeDtypeStruct((bsz, ML_HEADS, LANE), F32)]
    out_specs = [pl.BlockSpec((1, L, ML_V_W), lambda b, c: (b, c, 0)),
                 pl.BlockSpec((1, ML_HEADS, ML_DV, ML_DK), lambda b, c: (b, 0, 0, 0)),
                 pl.BlockSpec((1, ML_HEADS, ML_DK), lambda b, c: (b, 0, 0)),
                 pl.BlockSpec((1, ML_HEADS, LANE), lambda b, c: (b, 0, 0))]
    b_flat = b_if.reshape(2 * ML_HEADS).astype(F32)
    y, cN, nN, mN = pl.pallas_call(
        functools.partial(_mlstm_kernel, L=L),
        out_shape=out_shape, grid=(bsz, nc), in_specs=in_specs, out_specs=out_specs,
        compiler_params=_cparams(("parallel", "arbitrary")),
        name="mlstm_mixer",
    )(u, u, u, u, u, gc, gt, b_flat.reshape(1, -1), b_flat.reshape(-1, 1),
      norm_g.reshape(1, ML_V_W).astype(F32), c0, n0, m0)
    return y, cN, nN, mN[:, :, 0]


ML_ST_W = ML_DV + LANE
ML_EXT_ROWS = 16


def _split3(x):
    x1 = x.astype(jnp.bfloat16).astype(F32)
    r1 = x - x1
    x2 = r1.astype(jnp.bfloat16).astype(F32)
    x3 = (r1 - x2).astype(jnp.bfloat16).astype(F32)
    return x1, x2, x3


def _gate_scan_kernel(g_ref, br_ref, o_ref, *, L):
    H = ML_HEADS
    g = g_ref[...] + br_ref[...]
    li = g[0:H]
    pos = lax.rem(lax.broadcasted_iota(jnp.int32, li.shape, 1), L)
    b = _log_sigmoid(g[H:2 * H])
    d = 1
    while d < L:
        b = b + jnp.where(pos >= d, pltpu.roll(b, d, 1), 0.0)
        d *= 2
    beta0 = li - b
    cm = beta0
    d = 1
    while d < L:
        cm = jnp.maximum(cm, jnp.where(pos >= d, pltpu.roll(cm, d, 1), -jnp.inf))
        d *= 2
    o_ref[0:H, :] = beta0
    o_ref[H:2 * H, :] = b
    o_ref[2 * H:3 * H, :] = cm


def _gate_scan(gt, b_if, *, L, tb):
    m = gt.shape[1]
    return pl.pallas_call(
        functools.partial(_gate_scan_kernel, L=L),
        out_shape=jax.ShapeDtypeStruct((3 * ML_HEADS, m), F32), grid=(m // tb,),
        in_specs=[pl.BlockSpec((2 * ML_HEADS, tb), lambda i: (0, i)),
                  pl.BlockSpec((2 * ML_HEADS, 1), lambda i: (0, 0))],
        out_specs=pl.BlockSpec((3 * ML_HEADS, tb), lambda i: (0, i)),
        compiler_params=_cparams(("parallel",)),
        name="mlstm_gate_scan",
    )(gt, b_if.reshape(2 * ML_HEADS, 1).astype(F32))


def _mlstm_long_kernel(q_ref, k_ref, v_ref, o_ref, z_ref, gs_ref, ng_ref, c0_ref, n0_ref, m0_ref,
                       y_ref, cs_ref, ns_ref, ms_ref, st_sc, *, L):
    c = pl.program_id(1)
    nc = pl.num_programs(1)
    H = ML_HEADS
    bf = jnp.bfloat16

    @pl.when(c == 0)
    def _():
        for h in range(H):
            st_sc[h, :, 0:ML_DV] = c0_ref[0, h].T
            st_sc[h, :, ML_DV:ML_ST_W] = jnp.broadcast_to(n0_ref[0, h:h + 1, :], (ML_DK, LANE)).T
        ms_ref[...] = m0_ref[...]

    row = lax.broadcasted_iota(jnp.int32, (L, L), 0)
    col = lax.broadcasted_iota(jnp.int32, (L, L), 1)
    tril = col <= row

    beta0 = gs_ref[0:H, :]
    b = gs_ref[H:2 * H, :]
    cm = gs_ref[2 * H:3 * H, :]
    m_old = ms_ref[0]
    mm = jnp.maximum(jnp.concatenate([m_old] * (L // LANE), axis=1), cm)
    a3 = _split3(-mm)
    b3 = _split3(beta0 + jnp.log(jnp.float32(ML_DK ** -0.5)))
    n3 = _split3(-b)
    o3 = _split3(m_old)

    ones_l = jnp.ones((3, L), F32)
    ones_s = jnp.ones((3, LANE), F32)
    sub = lax.broadcasted_iota(jnp.int32, (ML_EXT_ROWS, LANE), 0)
    rhs_f = jnp.where((sub < 3) | ((sub >= 6) & (sub < 9)), 1.0, 0.0)
    ones_v = jnp.ones((L, LANE), MXU_DTYPE)
    mean_w = jnp.full((ML_DV, LANE), 1.0 / ML_DV, MXU_DTYPE)

    def front(h):
        hs = slice(h, h + 1)
        lhs = jnp.concatenate([a3[0][hs], a3[1][hs], a3[2][hs], ones_l, n3[0][hs], n3[1][hs], n3[2][hs],
                               jnp.zeros((ML_EXT_ROWS - 9, L), F32)], axis=0)
        rhs_e = jnp.concatenate([ones_l, b3[0][hs], b3[1][hs], b3[2][hs],
                                 jnp.zeros((ML_EXT_ROWS - 6, L), F32)], axis=0)
        rhs_g = jnp.concatenate([ones_s, o3[0][hs], o3[1][hs], o3[2][hs],
                                 jnp.zeros((ML_EXT_ROWS - 6, LANE), F32)], axis=0)
        rhs = jnp.concatenate([rhs_g, rhs_f, rhs_e], axis=1).astype(bf)
        ext = _tn_dot(lhs.astype(bf), rhs)
        ef = ext[:, LANE:2 * LANE]
        wg = jnp.exp(ext[:, 0:LANE])
        fl = jnp.exp(ef)
        p = jnp.exp(jnp.where(tril, ext[:, 2 * LANE:], -jnp.inf))

        q = q_ref[0, :, h * ML_DK:(h + 1) * ML_DK]
        k = k_ref[0, :, h * ML_DK:(h + 1) * ML_DK]
        v = v_ref[0, :, h * ML_DV:(h + 1) * ML_DV]
        s = p * _nt_dot(q, k)
        qw = q.astype(F32) * wg
        x = jnp.concatenate([qw.astype(MXU_DTYPE), s.astype(MXU_DTYPE)], axis=1)
        vo = jnp.concatenate([v.astype(MXU_DTYPE), ones_v], axis=1)
        st = st_sc[h]
        res = jnp.dot(x, jnp.concatenate([st.astype(MXU_DTYPE), vo], axis=0), preferred_element_type=F32)

        ktw = (k.astype(F32).T * p[L - 1:L, :]).astype(MXU_DTYPE)
        w_old = wg[L - 1:L, :]
        st_sc[h] = st * jnp.concatenate([w_old] * (ML_ST_W // LANE), axis=1) + jnp.dot(
            ktw, vo, preferred_element_type=F32)
        ms_ref[0, hs, :] = -ef[L - 1:L, :]
        return res, fl

    def back(h, res, fl):
        r = 1.0 / jnp.maximum(jnp.abs(res[:, ML_DV:]), fl)
        hh = res[:, :ML_DV] * jnp.concatenate([r, r], axis=1)
        msq = jnp.dot((hh * hh).astype(MXU_DTYPE), mean_w, preferred_element_type=F32)
        rs = lax.rsqrt(msq + EPS)
        hn = hh * jnp.concatenate([rs, rs], axis=1) * ng_ref[:, h * ML_DV:(h + 1) * ML_DV]
        og = o_ref[0, :, h * ML_DV:(h + 1) * ML_DV].astype(F32)
        zg = z_ref[0, :, h * ML_DV:(h + 1) * ML_DV].astype(F32)
        gate = zg * (1.0 + jnp.tanh(0.5 * og)) * (1.0 + jnp.tanh(0.5 * zg))
        y_ref[0, :, h * ML_DV:(h + 1) * ML_DV] = (hn * gate).astype(y_ref.dtype)

    pending = None
    for h in range(H + 1):
        nxt = front(h) if h < H else None
        if pending is not None:
            back(h - 1, *pending)
        pending = nxt

    @pl.when(c == nc - 1)
    def _():
        for h in range(H):
            cs_ref[0, h] = st_sc[h, :, 0:ML_DV].T
            ns_ref[0, h:h + 1, :] = st_sc[h, :, ML_DV:ML_ST_W].T[0:1, :]


def _mlstm_mixer_long(u, gscan, norm_g, c0, n0, m0, *, L):
    bsz, s, _ = u.shape
    nc = s // L
    shared = c0.shape[0] == 1
    st = (lambda b, c: (0, 0, 0, 0)) if shared else (lambda b, c: (b, 0, 0, 0))
    st3 = (lambda b, c: (0, 0, 0)) if shared else (lambda b, c: (b, 0, 0))
    in_specs = [
        pl.BlockSpec((1, L, ML_QK_W), lambda b, c: (b, c, 0)),
        pl.BlockSpec((1, L, ML_QK_W), lambda b, c: (b, c, 1)),
        pl.BlockSpec((1, L, ML_V_W), lambda b, c: (b, c, 1)),
        pl.BlockSpec((1, L, ML_V_W), lambda b, c: (b, c, 2)),
        pl.BlockSpec((1, L, ML_V_W), lambda b, c: (b, c, 3)),
        pl.BlockSpec((3 * ML_HEADS, L), lambda b, c, nc=nc: (0, b * nc + c)),
        pl.BlockSpec((1, ML_V_W), lambda b, c: (0, 0)),
        pl.BlockSpec((1, ML_HEADS, ML_DV, ML_DK), st),
        pl.BlockSpec((1, ML_HEADS, ML_DK), st3),
        pl.BlockSpec((1, ML_HEADS, LANE), st3),
    ]
    out_shape = [jax.ShapeDtypeStruct((bsz, s, ML_V_W), ACT_DTYPE),
                 jax.ShapeDtypeStruct((bsz, ML_HEADS, ML_DV, ML_DK), F32),
                 jax.ShapeDtypeStruct((bsz, ML_HEADS, ML_DK), F32),
                 jax.ShapeDtypeStruct((bsz, ML_HEADS, LANE), F32)]
    out_specs = [pl.BlockSpec((1, L, ML_V_W), lambda b, c: (b, c, 0)),
                 pl.BlockSpec((1, ML_HEADS, ML_DV, ML_DK), lambda b, c: (b, 0, 0, 0)),
                 pl.BlockSpec((1, ML_HEADS, ML_DK), lambda b, c: (b, 0, 0)),
                 pl.BlockSpec((1, ML_HEADS, LANE), lambda b, c: (b, 0, 0))]
    y, cN, nN, mN = pl.pallas_call(
        functools.partial(_mlstm_long_kernel, L=L),
        out_shape=out_shape, grid=(bsz, nc), in_specs=in_specs, out_specs=out_specs,
        scratch_shapes=[pltpu.VMEM((ML_HEADS, ML_DK, ML_ST_W), F32)],
        compiler_params=_cparams(("parallel", "arbitrary")),
        name="mlstm_mixer_long",
    )(u, u, u, u, u, gscan, 0.25 * norm_g.reshape(1, ML_V_W).astype(F32), c0, n0, m0)
    return y, cN, nN, mN[:, :, 0]


LOG2E = 1.4426950408889634


def _swa_kernel(*refs, Lc, G, n_hist, n_invalid):
    sink_ref, q_ref, z_ref, kv_ref = refs[:4]
    pos = 4
    if n_hist:
        hist_ref = refs[pos]
        prev_refs = refs[pos + 1:pos + 1 + n_hist]
        pos += 1 + n_hist
    y_ref = refs[pos]
    c = pl.program_id(1)
    nk = (n_hist + 1) * Lc
    pairs = SW_HEADS // SW_KV // 2
    rows = pairs * Lc

    kv = kv_ref[0]
    if n_hist:
        prev = jnp.concatenate([r[0] for r in prev_refs], axis=0)
        kv = jnp.concatenate([jnp.where(c == 0, hist_ref[0], prev), kv], axis=0)
    kv = kv.astype(F32)
    nkr = kv.shape[0]
    lane = lax.broadcasted_iota(jnp.int32, (nkr, LANE), 1)
    low = lane < SW_HD
    blk = lax.broadcasted_iota(jnp.int32, (rows, 1), 0) // Lc
    ones_v = jnp.ones((nkr, LANE), MXU_DTYPE)

    def split(t, g):
        swapped = pltpu.roll(t, SW_HD, 1)
        if g % 2 == 0:
            lo, hi = t, swapped
        else:
            lo, hi = swapped, t
        return (jnp.where(low, lo, 0.0).astype(MXU_DTYPE), jnp.where(low, 0.0, hi).astype(MXU_DTYPE))

    for g in range(SW_KV):
        kt = kv[:, (g // 2) * LANE:(g // 2 + 1) * LANE] * (SW_HD ** -0.5 * LOG2E)
        vt = kv[:, SW_KV_W + (g // 2) * LANE:SW_KV_W + (g // 2 + 1) * LANE]
        k_par = split(kt, g)
        v_par = tuple(jnp.concatenate([vv, ones_v], axis=1) for vv in split(vt, g))
        sinks = []
        for par in range(2):
            sk = jnp.zeros((rows, 1), F32)
            for j in range(pairs):
                sk = jnp.where(blk == j, sink_ref[SW_HEADS // SW_KV * g + 2 * j + par] * LOG2E, sk)
            sinks.append(sk)

        def scores(i, k_par=k_par):
            ks = slice(i * Lc, i * Lc + nk)
            qs = jnp.concatenate([q_ref[0, i * Lc:(i + 1) * Lc, (pairs * g + j) * LANE:(pairs * g + j + 1) * LANE]
                                  for j in range(pairs)], axis=0)
            ss = []
            for par in range(2):
                s = _nt_dot(qs, k_par[par][ks])
                if n_invalid > i * Lc:
                    key_idx = (c * G + i) * Lc + lax.broadcasted_iota(jnp.int32, (1, nk), 1)
                    s = jnp.where(key_idx >= n_invalid, s, -jnp.inf)
                ss.append(s)
            return ss

        def attend(i, ss, g=g, v_par=v_par, sinks=sinks):
            ks = slice(i * Lc, i * Lc + nk)
            out = jnp.zeros((rows, LANE), F32)
            for par in range(2):
                mx = jnp.maximum(jnp.max(ss[par], axis=1, keepdims=True), sinks[par])
                p = jnp.exp2(ss[par] - mx)
                res = jnp.dot(p.astype(MXU_DTYPE), v_par[par][ks], preferred_element_type=F32)
                den = res[:, LANE:] + jnp.exp2(sinks[par] - mx)
                out = out + res[:, :LANE] / den
            for j in range(pairs):
                cs = slice((pairs * g + j) * LANE, (pairs * g + j + 1) * LANE)
                zg = z_ref[0, i * Lc:(i + 1) * Lc, cs].astype(F32)
                y_ref[0, i * Lc:(i + 1) * Lc, cs] = (out[j * Lc:(j + 1) * Lc] * _silu(zg)).astype(y_ref.dtype)

        pending = None
        for i in range(G + 1):
            nxt = scores(i) if i < G else None
            if pending is not None:
                attend(i - 1, pending)
            pending = nxt


def _swa_mixer(u, hist, sinks, *, Lc, G, n_invalid):
    bsz, s, _ = u.shape
    tq = G * Lc
    nc = s // tq
    kv_blk = 2 * SW_W // (2 * SW_KV_W)
    n_hist = 0 if hist is None else hist.shape[1] // Lc
    in_specs = [pl.BlockSpec(memory_space=pltpu.SMEM),
                pl.BlockSpec((1, tq, SW_W), lambda b, c: (b, c, 0)),
                pl.BlockSpec((1, tq, SW_W), lambda b, c: (b, c, 1)),
                pl.BlockSpec((1, tq, 2 * SW_KV_W), lambda b, c: (b, c, kv_blk))]
    args = [sinks.astype(F32), u, u, u]
    if n_hist:
        hmap = (lambda b, c: (0, 0, 0)) if hist.shape[0] == 1 else (lambda b, c: (b, 0, 0))
        in_specs.append(pl.BlockSpec((1, n_hist * Lc, 2 * SW_KV_W), hmap))
        args.append(hist)
        for i in range(n_hist):
            in_specs.append(pl.BlockSpec(
                (1, Lc, 2 * SW_KV_W),
                lambda b, c, i=i: (b, jnp.maximum(c * G - n_hist + i, 0), kv_blk)))
            args.append(u)
    return pl.pallas_call(
        functools.partial(_swa_kernel, Lc=Lc, G=G, n_hist=n_hist, n_invalid=n_invalid),
        out_shape=jax.ShapeDtypeStruct((bsz, s, SW_W), ACT_DTYPE), grid=(bsz, nc),
        in_specs=in_specs, out_specs=pl.BlockSpec((1, tq, SW_W), lambda b, c: (b, c, 0)),
        compiler_params=_cparams(("parallel", "parallel")),
        name="swa_mixer",
    )(*args)


def _pool_kernel(u_ref, z_ref, h0_ref, wg_ref, sc_ref, y_ref, e_sc, *, T, from_start):
    t = pl.program_id(1)

    @pl.when(t == 0)
    def _():
        e_sc[0:POOL_HIST, :] = h0_ref[0]

    u = u_ref[0].astype(F32)
    e_sc[POOL_HIST:POOL_HIST + T, :] = u
    for g, w in enumerate(POOL_WINDOWS):
        cs = slice(g * POOL_GC, (g + 1) * POOL_GC)
        ug = u[:, cs]
        acc = ug
        for d in range(1, w):
            acc = acc + e_sc[POOL_HIST - d:POOL_HIST - d + T, cs]
        if from_start:
            pos = t * T + lax.broadcasted_iota(jnp.int32, (T, 1), 0) + 1
            pooled = acc / jnp.minimum(pos, w).astype(F32)
        else:
            pooled = acc * (1.0 / w)
        pooled = pooled - ug
        mixed = jnp.dot(pooled.astype(MXU_DTYPE), wg_ref[g], preferred_element_type=F32)
        zg = z_ref[0, :, cs].astype(F32)
        y_ref[0, :, cs] = (mixed * sc_ref[:, cs] * _silu(zg)).astype(y_ref.dtype)
    e_sc[0:POOL_HIST, :] = e_sc[T:T + POOL_HIST, :]


def _pool_mixer(u, hist, w_grp, scale, *, T, from_start):
    bsz, s, w2 = u.shape
    w = w2 // 2
    nt = s // T
    hmap = (lambda b, t: (0, 0, 0)) if hist.shape[0] == 1 else (lambda b, t: (b, 0, 0))
    return pl.pallas_call(
        functools.partial(_pool_kernel, T=T, from_start=from_start),
        out_shape=jax.ShapeDtypeStruct((bsz, s, w), ACT_DTYPE), grid=(bsz, nt),
        in_specs=[pl.BlockSpec((1, T, w), lambda b, t: (b, t, 0)),
                  pl.BlockSpec((1, T, w), lambda b, t: (b, t, 1)),
                  pl.BlockSpec((1, POOL_HIST, w), hmap),
                  pl.BlockSpec((len(POOL_WINDOWS), POOL_GC, POOL_GC), lambda b, t: (0, 0, 0)),
                  pl.BlockSpec((1, w), lambda b, t: (0, 0))],
        out_specs=pl.BlockSpec((1, T, w), lambda b, t: (b, t, 0)),
        scratch_shapes=[pltpu.VMEM((POOL_HIST + T, w), F32)],
        compiler_params=_cparams(("parallel", "arbitrary")),
        name="pool_mixer",
    )(u, u, hist, w_grp, scale.reshape(1, w).astype(F32))


def _pool_band_matrices(T):
    t = np.arange(T)[:, None]
    k = np.arange(T)[None, :]
    th = np.arange(POOL_HIST)[:, None]
    ph = np.arange(POOL_HIST)[None, :] - POOL_HIST
    main = [((k <= t) & (k > t - w)) / w - (k == t) for w in POOL_WINDOWS]
    hist = [(ph > th - w) / w for w in POOL_WINDOWS]
    return np.stack(main).astype(np.float32), np.stack(hist).astype(np.float32)


def _pool_long_kernel(u_ref, z_ref, h0_ref, wg_ref, sc_ref, bm_ref, bh_ref, y_ref, hist_sc, *, T):
    t = pl.program_id(1)

    @pl.when(t == 0)
    def _():
        hist_sc[...] = h0_ref[0]

    for g in range(len(POOL_WINDOWS)):
        cs = slice(g * POOL_GC, (g + 1) * POOL_GC)
        pooled = jnp.dot(bm_ref[g], u_ref[0, :, cs].astype(MXU_DTYPE), preferred_element_type=F32)
        head = pooled[0:POOL_HIST] + jnp.dot(bh_ref[g], hist_sc[:, cs].astype(MXU_DTYPE),
                                             preferred_element_type=F32)
        pooled = jnp.concatenate([head, pooled[POOL_HIST:]], axis=0)
        mixed = jnp.dot(pooled.astype(MXU_DTYPE), wg_ref[g], preferred_element_type=F32)
        zg = z_ref[0, :, cs].astype(F32)
        y_ref[0, :, cs] = (mixed * sc_ref[:, cs] * _silu(zg)).astype(y_ref.dtype)
    hist_sc[...] = u_ref[0, T - POOL_HIST:T, 0:hist_sc.shape[1]]


def _pool_mixer_long(u, hist, w_grp, scale, *, T):
    assert jnp.dtype(ACT_DTYPE).itemsize <= jnp.dtype(MXU_DTYPE).itemsize
    bsz, s, w2 = u.shape
    w = w2 // 2
    nt = s // T
    ng = len(POOL_WINDOWS)
    bm, bh = _pool_band_matrices(T)
    hmap = (lambda b, t: (0, 0, 0)) if hist.shape[0] == 1 else (lambda b, t: (b, 0, 0))
    return pl.pallas_call(
        functools.partial(_pool_long_kernel, T=T),
        out_shape=jax.ShapeDtypeStruct((bsz, s, w), ACT_DTYPE), grid=(bsz, nt),
        in_specs=[pl.BlockSpec((1, T, w), lambda b, t: (b, t, 0)),
                  pl.BlockSpec((1, T, w), lambda b, t: (b, t, 1)),
                  pl.BlockSpec((1, POOL_HIST, w), hmap),
                  pl.BlockSpec((ng, POOL_GC, POOL_GC), lambda b, t: (0, 0, 0)),
                  pl.BlockSpec((1, w), lambda b, t: (0, 0)),
                  pl.BlockSpec((ng, T, T), lambda b, t: (0, 0, 0)),
                  pl.BlockSpec((ng, POOL_HIST, POOL_HIST), lambda b, t: (0, 0, 0))],
        out_specs=pl.BlockSpec((1, T, w), lambda b, t: (b, t, 0)),
        scratch_shapes=[pltpu.VMEM((POOL_HIST, w), ACT_DTYPE)],
        compiler_params=_cparams(("parallel", "arbitrary")),
        name="pool_mixer_long",
    )(u, u, hist, w_grp, scale.reshape(1, w).astype(F32),
      jnp.asarray(bm, MXU_DTYPE), jnp.asarray(bh, MXU_DTYPE))


def _rope_tables(pos):
    half = ROT_DIM // 2
    inv = np.power(ROPE_THETA, -np.arange(half, dtype=np.float64) / half)
    ang = np.asarray(pos, np.float64)[:, None] * inv[None, :]
    cos, sin = np.cos(ang), np.sin(ang)
    pad = np.zeros((ang.shape[0], SW_HD - ROT_DIM))
    ta = np.concatenate([cos, cos, pad + 1.0], axis=1)
    ts = np.concatenate([-sin, sin, pad], axis=1)
    return tuple(jnp.asarray(np.concatenate([t, t], axis=1), F32) for t in (ta, ts))


def _rope_partner_matrix():
    half = ROT_DIM // 2
    p = np.zeros((LANE, LANE), np.float32)
    for d in range(LANE):
        if d % SW_HD < half:
            p[d + half, d] = 1.0
        elif d % SW_HD < ROT_DIM:
            p[d - half, d] = 1.0
    return p


def _pick_tile(m, pref):
    t = pref
    while m % t:
        t //= 2
    return t


def kernel(x_prompt, x_sample, state_mlstm_C, state_mlstm_n, state_mlstm_m, cache_swa_k, cache_swa_v, state_pool, meta_tokens, norm_g, final_norm_g, mlstm_w_in, mlstm_b_if, mlstm_norm_g, mlstm_w_out, swa_w_in, swa_sinks, swa_w_out, pool_w_in, pool_w_grp, pool_scale, pool_w_out):
    bp, sp, d = x_prompt.shape
    bs, ss, _ = x_sample.shape
    depth = norm_g.shape[0]
    mp = bp * sp
    ms_rows = bs * ss
    small = ms_rows + N_META

    xp = x_prompt.reshape(mp, d)
    xs = jnp.concatenate([x_sample.reshape(ms_rows, d), meta_tokens.astype(x_prompt.dtype)], axis=0)
    tm_p = _pick_tile(sp, 1024)
    tm_o = _pick_tile(sp, 512)
    ml_chunk = _pick_tile(sp, 256)
    pool_tile = _pick_tile(sp, 256)

    pC, pn, pm, pk, pv, pp = [], [], [], [], [], []
    sC, sn, sm, sk, sv, s_pool = [], [], [], [], [], []
    for i in range(depth):
        kind, j = i % N_MIXERS, i // N_MIXERS
        last = i == depth - 1
        fin = final_norm_g if last else None
        if kind == 0:
            w_in = mlstm_w_in[j]
            w_t = w_in.T.astype(MXU_DTYPE)
            gate_w = jnp.pad(w_t[ML_MAIN_W:], ((0, ML_GATE_PAD - 2 * ML_HEADS), (0, 0)))
            up, gcp, gtp = _proj_in(xp, norm_g[i], w_t, tm=tm_p, tn=2048, n=ML_MAIN_W, gate_w=gate_w,
                                    w_transposed=True)
            us, gcs, gts = _proj_in(xs, norm_g[i], w_t, tm=small, tn=1024, n=ML_MAIN_W, gate_w=gate_w,
                                    w_transposed=True)
            zc = jnp.zeros((1, ML_HEADS, ML_DV, ML_DK), F32)
            zn = jnp.zeros((1, ML_HEADS, ML_DK), F32)
            zm = jnp.zeros((1, ML_HEADS, LANE), F32)
            ym, c_m, n_m, m_m = _mlstm_mixer(
                us[ms_rows:].reshape(1, N_META, -1), gcs[ms_rows:].reshape(1, N_META, -1),
                gts[:, ms_rows:].reshape(1, 1, 2 * ML_HEADS, N_META),
                mlstm_b_if[j], mlstm_norm_g[j], zc, zn, zm, L=N_META)
            ysm, c_s, n_s, m_s = _mlstm_mixer(
                us[:ms_rows].reshape(bs, ss, -1), gcs[:ms_rows].reshape(bs, ss, -1),
                gts[:, :ms_rows].reshape(2 * ML_HEADS, bs, 1, ss).transpose(1, 2, 0, 3),
                mlstm_b_if[j], mlstm_norm_g[j], state_mlstm_C[j].astype(F32), state_mlstm_n[j].astype(F32),
                jnp.broadcast_to(state_mlstm_m[j].astype(F32)[..., None], (bs, ML_HEADS, LANE)), L=ss)
            yp, c_p, n_p, m_p = _mlstm_mixer_long(
                up.reshape(bp, sp, -1), _gate_scan(gtp, mlstm_b_if[j], L=ml_chunk, tb=sp),
                mlstm_norm_g[j], c_m, n_m,
                jnp.broadcast_to(m_m[..., None], (1, ML_HEADS, LANE)), L=ml_chunk)
            pC.append(c_p); pn.append(n_p); pm.append(m_p)
            sC.append(c_s); sn.append(n_s); sm.append(m_s)
            w_out = mlstm_w_out[j]
        elif kind == 1:
            w_in = swa_w_in[j]
            wq, wk, wv, wz = jnp.split(w_in, [SW_W, SW_W + SW_KV_W, SW_W + 2 * SW_KV_W], axis=1)
            w_perm = jnp.concatenate([wq, wz, wk, wv], axis=1).astype(MXU_DTYPE)
            n_in = w_perm.shape[1]
            rope_mask = np.concatenate([np.ones(SW_W, bool), np.zeros(SW_W, bool),
                                        np.ones(SW_KV_W, bool), np.zeros(SW_KV_W, bool)])
            tabs_p = _rope_tables(N_META + np.arange(sp))
            pos_s = np.concatenate([np.tile(N_META + PAST_LEN + np.arange(ss), bs), np.arange(N_META)])
            tabs_s = _rope_tables(pos_s)
            tn_sw = n_in // 3
            up = _proj_in(xp, norm_g[i], w_perm, tm=tm_p, tn=tn_sw, rope_tabs=tabs_p, rope_mask=rope_mask)
            us = _proj_in(xs, norm_g[i], w_perm, tm=small, tn=tn_sw, rope_tabs=tabs_s, rope_mask=rope_mask)
            up = up.reshape(bp, sp, n_in)
            kv_s = us[:ms_rows, 2 * SW_W:].reshape(bs, ss, 2 * SW_KV_W)
            kv_m = us[ms_rows:, 2 * SW_W:].reshape(1, N_META, 2 * SW_KV_W)
            ym = _swa_mixer(us[ms_rows:].reshape(1, N_META, n_in), None, swa_sinks[j],
                            Lc=N_META, G=1, n_invalid=0)
            cache = jnp.concatenate([cache_swa_k[j].reshape(bs, WINDOW, SW_KV_W),
                                     cache_swa_v[j].reshape(bs, WINDOW, SW_KV_W)], axis=-1).astype(ACT_DTYPE)
            ysm = _swa_mixer(us[:ms_rows].reshape(bs, ss, n_in), cache, swa_sinks[j],
                             Lc=SW_CHUNK, G=ss // SW_CHUNK, n_invalid=0)
            hist = jnp.concatenate([jnp.zeros((1, WINDOW - N_META, 2 * SW_KV_W), ACT_DTYPE), kv_m], axis=1)
            yp = _swa_mixer(up, hist, swa_sinks[j], Lc=SW_CHUNK, G=_pick_tile(sp // SW_CHUNK, 8),
                            n_invalid=WINDOW - N_META)
            kv_p = up[:, -WINDOW:, 2 * SW_W:]
            pk.append(kv_p[:, :, :SW_KV_W].astype(F32).reshape(bp, WINDOW, SW_KV, SW_HD))
            pv.append(kv_p[:, :, SW_KV_W:].astype(F32).reshape(bp, WINDOW, SW_KV, SW_HD))
            k_new = kv_s[:, :, :SW_KV_W].astype(F32).reshape(bs, ss, SW_KV, SW_HD)
            v_new = kv_s[:, :, SW_KV_W:].astype(F32).reshape(bs, ss, SW_KV, SW_HD)
            sk.append(jnp.concatenate([cache_swa_k[j].astype(F32), k_new], axis=1)[:, -WINDOW:])
            sv.append(jnp.concatenate([cache_swa_v[j].astype(F32), v_new], axis=1)[:, -WINDOW:])
            w_out = swa_w_out[j]
        else:
            w_in = pool_w_in[j].astype(MXU_DTYPE)
            pw = w_in.shape[1] // 2
            w_grp = pool_w_grp[j].astype(MXU_DTYPE)
            up = _proj_in(xp, norm_g[i], w_in, tm=tm_p, tn=2048).reshape(bp, sp, 2 * pw)
            us = _proj_in(xs, norm_g[i], w_in, tm=small, tn=1024)
            u_m = us[ms_rows:].reshape(1, N_META, 2 * pw)
            u_s = us[:ms_rows].reshape(bs, ss, 2 * pw)
            ym = _pool_mixer(u_m, jnp.zeros((1, POOL_HIST, pw), F32), w_grp, pool_scale[j],
                             T=N_META, from_start=True)
            hist_s = jnp.pad(state_pool[j].astype(F32), ((0, 0), (POOL_HIST - state_pool.shape[2], 0), (0, 0)))
            ysm = _pool_mixer(u_s, hist_s, w_grp, pool_scale[j], T=ss, from_start=False)
            yp = _pool_mixer_long(up, u_m[:, :, :pw], w_grp, pool_scale[j], T=pool_tile)
            n_keep = state_pool.shape[2]
            pp.append(up[:, -n_keep:, :pw].astype(F32))
            s_pool.append(u_s[:, -n_keep:, :pw].astype(F32))
            w_out = pool_w_out[j]
        w_out = w_out.astype(MXU_DTYPE)
        y_small = jnp.concatenate([ysm.reshape(ms_rows, -1), ym.reshape(N_META, -1)], axis=0)
        xp = _proj_out(xp, yp.reshape(mp, -1), w_out, tm=tm_o, final_g=fin)
        xs = _proj_out(xs, y_small, w_out, tm=small, final_g=fin)

    y_prompt = xp.reshape(bp, sp, d)
    y_sample = xs[:ms_rows].reshape(bs, ss, d)
    return (y_prompt, y_sample,
            jnp.stack(pC), jnp.stack(pn), jnp.stack(pm), jnp.stack(pk), jnp.stack(pv), jnp.stack(pp),
            jnp.stack(sC), jnp.stack(sn), jnp.stack(sm), jnp.stack(sk), jnp.stack(sv), jnp.stack(s_pool))
```

```python
import functools

import numpy as np
import jax
import jax.numpy as jnp
from jax import lax
from jax.experimental import pallas as pl
from jax.experimental.pallas import tpu as pltpu

F32 = jnp.float32
MXU_DTYPE = jnp.bfloat16
ACT_DTYPE = jnp.bfloat16

EPS = 1e-6
N_META = 16
N_MIXERS = 3
PAST_LEN = 1024

ML_HEADS = 8
ML_DK = 128
ML_DV = 256
ML_QK_W = ML_HEADS * ML_DK
ML_V_W = ML_HEADS * ML_DV
ML_MAIN_W = 2 * ML_QK_W + 3 * ML_V_W
ML_GATE_PAD = 128

SW_HEADS = 32
SW_KV = 4
SW_HD = 64
SW_W = SW_HEADS * SW_HD
SW_KV_W = SW_KV * SW_HD
WINDOW = 128
SW_CHUNK = 64
ROT_DIM = 16
ROPE_THETA = 500000.0

POOL_WINDOWS = (2, 4, 8, 16)
POOL_GC = 512
POOL_HIST = 16

LANE = 128
VMEM_LIMIT_BYTES = 56 * 1024 * 1024


def _cparams(sem):
    return pltpu.CompilerParams(dimension_semantics=sem, vmem_limit_bytes=VMEM_LIMIT_BYTES)


def _nt_dot(a, b):
    return lax.dot_general(a, b, (((1,), (1,)), ((), ())), preferred_element_type=F32)


def _tn_dot(a, b):
    return lax.dot_general(a, b, (((0,), (0,)), ((), ())), preferred_element_type=F32)


def _sigmoid(x):
    return 0.5 + 0.5 * jnp.tanh(0.5 * x)


def _silu(x):
    return x * _sigmoid(x)


def _log_sigmoid(x):
    return jnp.minimum(x, 0.0) - jnp.log1p(jnp.exp(-jnp.abs(x)))


def _proj_in_kernel(*refs, rope, gates, w_transposed):
    x_ref, g_ref, w_ref = refs[:3]
    pos = 3
    if rope:
        ta_ref, ts_ref, perm_ref = refs[pos:pos + 3]
        pos += 3
    if gates:
        wg_ref = refs[pos]
        pos += 1
    o_ref = refs[pos]
    pos += 1
    if gates:
        gc_ref, gt_ref = refs[pos:pos + 2]
        pos += 2
    hn_sc = refs[pos]
    j = pl.program_id(1)

    @pl.when(j == 0)
    def _():
        x = x_ref[...]
        ms = jnp.mean(x * x, axis=1, keepdims=True)
        hn = (x * lax.rsqrt(ms + EPS) * g_ref[...]).astype(MXU_DTYPE)
        hn_sc[...] = hn
        if gates:
            gcol = _nt_dot(hn, wg_ref[...])
            gc_ref[...] = gcol[:, :2 * ML_HEADS]
            gt_ref[...] = gcol.T[:2 * ML_HEADS, :]

    if w_transposed:
        acc = _nt_dot(hn_sc[...], w_ref[...])
    else:
        acc = jnp.dot(hn_sc[...], w_ref[...], preferred_element_type=F32)
    if not rope:
        o_ref[...] = acc.astype(o_ref.dtype)
        return

    def store(pattern):
        pieces = []
        for l, rotate in enumerate(pattern):
            a = acc[:, l * LANE:(l + 1) * LANE]
            if rotate:
                partner = jnp.dot(a.astype(MXU_DTYPE), perm_ref[...], preferred_element_type=F32)
                a = a * ta_ref[...] + partner * ts_ref[...]
            pieces.append(a.astype(o_ref.dtype))
        o_ref[...] = jnp.concatenate(pieces, axis=1)

    for pattern in sorted(set(rope)):
        tiles = [t for t, p in enumerate(rope) if p == pattern]
        cond = functools.reduce(jnp.logical_or, [j == t for t in tiles])
        pl.when(cond)(functools.partial(store, pattern))


def _proj_in(x, g, w, *, tm, tn, n=None, rope_tabs=None, rope_mask=None, gate_w=None, w_transposed=False,
             w_layer=None):
    m, d = x.shape
    if n is None:
        n = w.shape[0] if w_transposed else w.shape[1]
    rope = ()
    if rope_tabs is not None:
        groups = rope_mask.reshape(n // LANE, LANE)
        assert (groups == groups[:, :1]).all()
        rope = tuple(tuple(bool(f) for f in groups[t * tn // LANE:(t + 1) * tn // LANE, 0])
                     for t in range(n // tn))
    gates = gate_w is not None
    grid = (m // tm, n // tn)
    in_specs = [pl.BlockSpec((tm, d), lambda i, j: (i, 0)),
                pl.BlockSpec((1, d), lambda i, j: (0, 0)),
                pl.BlockSpec((None, tn, d), lambda i, j: (w_layer, j, 0)) if w_layer is not None
                else pl.BlockSpec((tn, d), lambda i, j: (j, 0)) if w_transposed
                else pl.BlockSpec((d, tn), lambda i, j: (0, j))]
    args = [x, g.reshape(1, d), w]
    if rope:
        nrep = rope_tabs[0].shape[0] // tm
        for t in rope_tabs:
            in_specs.append(pl.BlockSpec((tm, LANE), lambda i, j, nrep=nrep: (i % nrep, 0)))
            args.append(t)
        in_specs.append(pl.BlockSpec((LANE, LANE), lambda i, j: (0, 0)))
        args.append(jnp.asarray(_rope_partner_matrix(), MXU_DTYPE))
    if gates:
        in_specs.append(pl.BlockSpec((ML_GATE_PAD, d), lambda i, j: (0, 0)))
        args.append(gate_w)
    out_shape = [jax.ShapeDtypeStruct((m, n), ACT_DTYPE)]
    out_specs = [pl.BlockSpec((tm, tn), lambda i, j: (i, j))]
    if gates:
        out_shape += [jax.ShapeDtypeStruct((m, 2 * ML_HEADS), F32),
                      jax.ShapeDtypeStruct((2 * ML_HEADS, m), F32)]
        out_specs += [pl.BlockSpec((tm, 2 * ML_HEADS), lambda i, j: (i, 0)),
                      pl.BlockSpec((2 * ML_HEADS, tm), lambda i, j: (0, i))]
    res = pl.pallas_call(
        functools.partial(_proj_in_kernel, rope=rope, gates=gates, w_transposed=w_transposed),
        out_shape=out_shape, grid=grid, in_specs=in_specs, out_specs=out_specs,
        scratch_shapes=[pltpu.VMEM((tm, d), MXU_DTYPE)],
        compiler_params=_cparams(("parallel", "arbitrary")),
        name="proj_in",
    )(*args)
    return res if gates else res[0]


def _proj_out_kernel(*refs, final_norm):
    y_ref, w_ref, x_ref = refs[:3]
    if final_norm:
        g_ref, o_ref = refs[3:5]
    else:
        o_ref = refs[3]
    acc = x_ref[...] + jnp.dot(y_ref[...].astype(MXU_DTYPE), w_ref[...], preferred_element_type=F32)
    if final_norm:
        ms = jnp.mean(acc * acc, axis=1, keepdims=True)
        acc = acc * lax.rsqrt(ms + EPS) * g_ref[...]
    o_ref[...] = acc


def _proj_out(x, y, w, *, tm, final_g=None):
    m, d = x.shape
    k = y.shape[1]
    final_norm = final_g is not None
    in_specs = [pl.BlockSpec((tm, k), lambda i: (i, 0)),
                pl.BlockSpec((k, d), lambda i: (0, 0)),
                pl.BlockSpec((tm, d), lambda i: (i, 0))]
    args = [y, w, x]
    if final_norm:
        in_specs.append(pl.BlockSpec((1, d), lambda i: (0, 0)))
        args.append(final_g.reshape(1, d))
    return pl.pallas_call(
        functools.partial(_proj_out_kernel, final_norm=final_norm),
        out_shape=jax.ShapeDtypeStruct((m, d), F32), grid=(m // tm,),
        in_specs=in_specs, out_specs=pl.BlockSpec((tm, d), lambda i: (i, 0)),
        compiler_params=_cparams(("parallel",)),
        name="proj_out",
    )(*args)


def _mlstm_kernel(q_ref, k_ref, v_ref, o_ref, z_ref, gc_ref, gt_ref, bc_ref, br_ref, ng_ref,
                  c0_ref, n0_ref, m0_ref, y_ref, cs_ref, ns_ref, ms_ref, *, L):
    c = pl.program_id(1)

    @pl.when(c == 0)
    def _():
        cs_ref[...] = c0_ref[...]
        ns_ref[...] = n0_ref[...]
        ms_ref[...] = m0_ref[...]

    row = lax.broadcasted_iota(jnp.int32, (L, L), 0)
    col = lax.broadcasted_iota(jnp.int32, (L, L), 1)
    tril = col <= row
    scale = ML_DK ** -0.5

    gc = gc_ref[0] + bc_ref[...]
    lane16 = lax.broadcasted_iota(jnp.int32, gc.shape, 1)
    gc = jnp.where(lane16 < ML_HEADS, gc, _log_sigmoid(gc))
    gt = gt_ref[0, 0] + br_ref[...]
    sub16 = lax.broadcasted_iota(jnp.int32, gt.shape, 0)
    gt = jnp.where(sub16 < ML_HEADS, gt, _log_sigmoid(gt))

    for h in range(ML_HEADS):
        li_col = gc[:, h:h + 1]
        lf_col = gc[:, ML_HEADS + h:ML_HEADS + h + 1]
        li_row = gt[h:h + 1, :]
        lf_row = gt[ML_HEADS + h:ML_HEADS + h + 1, :]
        q = q_ref[0, :, h * ML_DK:(h + 1) * ML_DK]
        k = k_ref[0, :, h * ML_DK:(h + 1) * ML_DK]
        v = v_ref[0, :, h * ML_DV:(h + 1) * ML_DV]
        c_old = cs_ref[0, h]
        n_old = ns_ref[0, h:h + 1, :]
        m_old = ms_ref[0, h:h + 1, 0:1]

        b_col = jnp.sum(jnp.where(tril, lf_row, 0.0), axis=1, keepdims=True)
        b_row = jnp.sum(jnp.where(row <= col, lf_col, 0.0), axis=0, keepdims=True)
        dmat = jnp.where(tril, b_col - b_row + li_row, -jnp.inf)
        inter = b_col + m_old
        m_t = jnp.maximum(inter, jnp.max(dmat, axis=1, keepdims=True))
        p = jnp.exp(dmat - m_t)
        w_inter = jnp.exp(inter - m_t)
        s = p * (_nt_dot(q, k) * scale)
        qf = q.astype(F32)
        kf = k.astype(F32)
        num = w_inter * _nt_dot(q, c_old.astype(MXU_DTYPE)) + jnp.dot(
            s.astype(MXU_DTYPE), v, preferred_element_type=F32)
        den = w_inter * jnp.sum(qf * n_old, axis=1, keepdims=True) + jnp.sum(s, axis=1, keepdims=True)
        hh = num / jnp.maximum(jnp.abs(den), jnp.exp(-m_t))

        hn = hh * lax.rsqrt(jnp.mean(hh * hh, axis=1, keepdims=True) + EPS)
        hn = hn * ng_ref[:, h * ML_DV:(h + 1) * ML_DV]
        og = o_ref[0, :, h * ML_DV:(h + 1) * ML_DV].astype(F32)
        zg = z_ref[0, :, h * ML_DV:(h + 1) * ML_DV].astype(F32)
        y_ref[0, :, h * ML_DV:(h + 1) * ML_DV] = (hn * _sigmoid(og) * _silu(zg)).astype(y_ref.dtype)

        b_end = jnp.sum(lf_row, axis=1, keepdims=True)
        a_end = b_end - b_col + li_col
        m_new = jnp.maximum(b_end + m_old, jnp.max(a_end, axis=0, keepdims=True))
        w_old = jnp.exp(b_end + m_old - m_new)
        w_s = jnp.exp(a_end - m_new)
        vw = (v.astype(F32) * w_s).astype(MXU_DTYPE)
        cs_ref[0, h] = w_old * c_old + _tn_dot(vw, k) * scale
        ns_ref[0, h:h + 1, :] = w_old * n_old + jnp.sum(kf * w_s, axis=0, keepdims=True) * scale
        ms_ref[0, h:h + 1, :] = jnp.broadcast_to(m_new, (1, LANE))


def _mlstm_mixer(u, gc, gt, b_if, norm_g, c0, n0, m0, *, L):
    bsz, s, _ = u.shape
    nc = s // L
    shared = c0.shape[0] == 1
    st = (lambda b, c: (0, 0, 0, 0)) if shared else (lambda b, c: (b, 0, 0, 0))
    st3 = (lambda b, c: (0, 0, 0)) if shared else (lambda b, c: (b, 0, 0))
    qk_b = ML_QK_W
    v_b = ML_V_W
    in_specs = [
        pl.BlockSpec((1, L, qk_b), lambda b, c: (b, c, 0)),
        pl.BlockSpec((1, L, qk_b), lambda b, c: (b, c, 1)),
        pl.BlockSpec((1, L, v_b), lambda b, c: (b, c, 1)),
        pl.BlockSpec((1, L, v_b), lambda b, c: (b, c, 2)),
        pl.BlockSpec((1, L, v_b), lambda b, c: (b, c, 3)),
        pl.BlockSpec((1, L, 2 * ML_HEADS), lambda b, c: (b, c, 0)),
        pl.BlockSpec((1, 1, 2 * ML_HEADS, L), lambda b, c: (b, c, 0, 0)),
        pl.BlockSpec((1, 2 * ML_HEADS), lambda b, c: (0, 0)),
        pl.BlockSpec((2 * ML_HEADS, 1), lambda b, c: (0, 0)),
        pl.BlockSpec((1, ML_V_W), lambda b, c: (0, 0)),
        pl.BlockSpec((1, ML_HEADS, ML_DV, ML_DK), st),
        pl.BlockSpec((1, ML_HEADS, ML_DK), st3),
        pl.BlockSpec((1, ML_HEADS, LANE), st3),
    ]
    out_shape = [jax.ShapeDtypeStruct((bsz, s, ML_V_W), ACT_DTYPE),
                 jax.ShapeDtypeStruct((bsz, ML_HEADS, ML_DV, ML_DK), F32),
                 jax.ShapeDtypeStruct((bsz, ML_HEADS, ML_DK), F32),
                 jax.ShapeDtypeStruct((bsz, ML_HEADS, LANE), F32)]
    out_specs = [pl.BlockSpec((1, L, ML_V_W), lambda b, c: (b, c, 0)),
                 pl.BlockSpec((1, ML_HEADS, ML_DV, ML_DK), lambda b, c: (b, 0, 0, 0)),
                 pl.BlockSpec((1, ML_HEADS, ML_DK), lambda b, c: (b, 0, 0)),
                 pl.BlockSpec((1, ML_HEADS, LANE), lambda b, c: (b, 0, 0))]
    b_flat = b_if.reshape(2 * ML_HEADS).astype(F32)
    y, cN, nN, mN = pl.pallas_call(
        functools.partial(_mlstm_kernel, L=L),
        out_shape=out_shape, grid=(bsz, nc), in_specs=in_specs, out_specs=out_specs,
        compiler_params=_cparams(("parallel", "arbitrary")),
        name="mlstm_mixer",
    )(u, u, u, u, u, gc, gt, b_flat.reshape(1, -1), b_flat.reshape(-1, 1),
      norm_g.reshape(1, ML_V_W).astype(F32), c0, n0, m0)
    return y, cN, nN, mN[:, :, 0]


ML_ST_W = ML_DV + LANE
ML_EXT_ROWS = 16


def _split3(x):
    x1 = x.astype(jnp.bfloat16).astype(F32)
    r1 = x - x1
    x2 = r1.astype(jnp.bfloat16).astype(F32)
    x3 = (r1 - x2).astype(jnp.bfloat16).astype(F32)
    return x1, x2, x3


def _gate_scan_kernel(g_ref, br_ref, o_ref, *, L):
    H = ML_HEADS
    g = g_ref[...] + br_ref[...]
    li = g[0:H]
    pos = lax.rem(lax.broadcasted_iota(jnp.int32, li.shape, 1), L)
    b = _log_sigmoid(g[H:2 * H])
    d = 1
    while d < L:
        b = b + jnp.where(pos >= d, pltpu.roll(b, d, 1), 0.0)
        d *= 2
    beta0 = li - b
    cm = beta0
    d = 1
    while d < L:
        cm = jnp.maximum(cm, jnp.where(pos >= d, pltpu.roll(cm, d, 1), -jnp.inf))
        d *= 2
    o_ref[0:H, :] = beta0
    o_ref[H:2 * H, :] = b
    o_ref[2 * H:3 * H, :] = cm


def _gate_scan(gt, b_if, *, L, tb):
    m = gt.shape[1]
    return pl.pallas_call(
        functools.partial(_gate_scan_kernel, L=L),
        out_shape=jax.ShapeDtypeStruct((3 * ML_HEADS, m), F32), grid=(m // tb,),
        in_specs=[pl.BlockSpec((2 * ML_HEADS, tb), lambda i: (0, i)),
                  pl.BlockSpec((2 * ML_HEADS, 1), lambda i: (0, 0))],
        out_specs=pl.BlockSpec((3 * ML_HEADS, tb), lambda i: (0, i)),
        compiler_params=_cparams(("parallel",)),
        name="mlstm_gate_scan",
    )(gt, b_if.reshape(2 * ML_HEADS, 1).astype(F32))


def _mlstm_long_kernel(q_ref, k_ref, v_ref, o_ref, z_ref, gs_ref, ng_ref, c0_ref, n0_ref, m0_ref,
                       y_ref, cs_ref, ns_ref, ms_ref, st_sc, *, L, CH):
    c = pl.program_id(1)
    nc = pl.num_programs(1)
    H = ML_HEADS
    bf = jnp.bfloat16

    @pl.when(c == 0)
    def _():
        for h in range(H):
            st_sc[h, :, 0:ML_DV] = c0_ref[0, h].T
            st_sc[h, :, ML_DV:ML_ST_W] = jnp.broadcast_to(n0_ref[0, h:h + 1, :], (ML_DK, LANE)).T
        ms_ref[...] = m0_ref[...]

    row = lax.broadcasted_iota(jnp.int32, (L, L), 0)
    col = lax.broadcasted_iota(jnp.int32, (L, L), 1)
    tril = col <= row

    ones_l = jnp.ones((3, L), F32)
    ones_s = jnp.ones((3, LANE), F32)
    sub = lax.broadcasted_iota(jnp.int32, (ML_EXT_ROWS, LANE), 0)
    rhs_f = jnp.where((sub < 3) | ((sub >= 6) & (sub < 9)), 1.0, 0.0)
    ones_v = jnp.ones((L, LANE), MXU_DTYPE)
    mean_w = jnp.full((ML_DV, LANE), 1.0 / ML_DV, MXU_DTYPE)

    def chunk_scalars(cc):
        ls = slice(cc * L, (cc + 1) * L)
        beta0 = gs_ref[0:H, ls]
        b = gs_ref[H:2 * H, ls]
        cm = gs_ref[2 * H:3 * H, ls]
        m_old = ms_ref[0]
        mm = jnp.maximum(jnp.concatenate([m_old] * (L // LANE), axis=1), cm)
        return (_split3(-mm * LOG2E), _split3((beta0 + jnp.log(jnp.float32(ML_DK ** -0.5))) * LOG2E),
                _split3(-b * LOG2E), _split3(m_old * LOG2E))

    def front(cc, h, scal):
        a3, b3, n3, o3 = scal
        hs = slice(h, h + 1)
        rs_ = slice(cc * L, (cc + 1) * L)
        lhs = jnp.concatenate([a3[0][hs], a3[1][hs], a3[2][hs], ones_l, n3[0][hs], n3[1][hs], n3[2][hs],
                               jnp.zeros((ML_EXT_ROWS - 9, L), F32)], axis=0)
        rhs_e = jnp.concatenate([ones_l, b3[0][hs], b3[1][hs], b3[2][hs],
                                 jnp.zeros((ML_EXT_ROWS - 6, L), F32)], axis=0)
        rhs_g = jnp.concatenate([ones_s, o3[0][hs], o3[1][hs], o3[2][hs],
                                 jnp.zeros((ML_EXT_ROWS - 6, LANE), F32)], axis=0)
        rhs = jnp.concatenate([rhs_g, rhs_f, rhs_e], axis=1).astype(bf)
        ext = _tn_dot(lhs.astype(bf), rhs)
        ef = ext[:, LANE:2 * LANE]
        wg = jnp.exp2(ext[:, 0:LANE])
        fl = jnp.exp2(ef)
        p = jnp.exp2(jnp.where(tril, ext[:, 2 * LANE:], -jnp.inf))

        q = q_ref[0, rs_, h * ML_DK:(h + 1) * ML_DK]
        k = k_ref[0, rs_, h * ML_DK:(h + 1) * ML_DK]
        v = v_ref[0, rs_, h * ML_DV:(h + 1) * ML_DV]
        s = p * _nt_dot(q, k)
        qw = q.astype(F32) * wg
        x = jnp.concatenate([qw.astype(MXU_DTYPE), s.astype(MXU_DTYPE)], axis=1)
        vo = jnp.concatenate([v.astype(MXU_DTYPE), ones_v], axis=1)
        st = st_sc[h]
        res = jnp.dot(x, jnp.concatenate([st.astype(MXU_DTYPE), vo], axis=0), preferred_element_type=F32)

        ktw = (k.astype(F32).T * p[L - 1:L, :]).astype(MXU_DTYPE)
        w_old = wg[L - 1:L, :]
        st_sc[h] = st * jnp.concatenate([w_old] * (ML_ST_W // LANE), axis=1) + jnp.dot(
            ktw, vo, preferred_element_type=F32)
        ms_ref[0, hs, :] = ef[L - 1:L, :] * (-1.0 / LOG2E)
        return res, fl

    def back(cc, h, res, fl):
        rs_ = slice(cc * L, (cc + 1) * L)
        r = 1.0 / jnp.maximum(jnp.abs(res[:, ML_DV:]), fl)
        hh = res[:, :ML_DV] * jnp.concatenate([r, r], axis=1)
        msq = jnp.dot((hh * hh).astype(MXU_DTYPE), mean_w, preferred_element_type=F32)
        rs = lax.rsqrt(msq + EPS)
        hn = hh * jnp.concatenate([rs, rs], axis=1) * ng_ref[:, h * ML_DV:(h + 1) * ML_DV]
        og = o_ref[0, rs_, h * ML_DV:(h + 1) * ML_DV].astype(F32)
        zg = z_ref[0, rs_, h * ML_DV:(h + 1) * ML_DV].astype(F32)
        gate = zg * (1.0 + jnp.tanh(0.5 * og)) * (1.0 + jnp.tanh(0.5 * zg))
        y_ref[0, rs_, h * ML_DV:(h + 1) * ML_DV] = (hn * gate).astype(y_ref.dtype)

    items = [(cc, h) for cc in range(CH) for h in range(H)]
    pending = None
    scal = None
    for n_item in range(len(items) + 1):
        nxt = None
        if n_item < len(items):
            cc, h = items[n_item]
            if h == 0:
                scal = chunk_scalars(cc)
            nxt = front(cc, h, scal)
        if pending is not None:
            back(*items[n_item - 1], *pending)
        pending = nxt

    @pl.when(c == nc - 1)
    def _():
        for h in range(H):
            cs_ref[0, h] = st_sc[h, :, 0:ML_DV].T
            ns_ref[0, h:h + 1, :] = st_sc[h, :, ML_DV:ML_ST_W].T[0:1, :]


def _mlstm_mixer_long(u, gscan, norm_g, c0, n0, m0, *, L):
    bsz, s, _ = u.shape
    CH = _pick_tile(s // L, 2)
    T = CH * L
    nc = s // T
    shared = c0.shape[0] == 1
    st = (lambda b, c: (0, 0, 0, 0)) if shared else (lambda b, c: (b, 0, 0, 0))
    st3 = (lambda b, c: (0, 0, 0)) if shared else (lambda b, c: (b, 0, 0))
    in_specs = [
        pl.BlockSpec((1, T, ML_QK_W), lambda b, c: (b, c, 0)),
        pl.BlockSpec((1, T, ML_QK_W), lambda b, c: (b, c, 1)),
        pl.BlockSpec((1, T, ML_V_W), lambda b, c: (b, c, 1)),
        pl.BlockSpec((1, T, ML_V_W), lambda b, c: (b, c, 2)),
        pl.BlockSpec((1, T, ML_V_W), lambda b, c: (b, c, 3)),
        pl.BlockSpec((3 * ML_HEADS, T), lambda b, c, nc=nc: (0, b * nc + c)),
        pl.BlockSpec((1, ML_V_W), lambda b, c: (0, 0)),
        pl.BlockSpec((1, ML_HEADS, ML_DV, ML_DK), st),
        pl.BlockSpec((1, ML_HEADS, ML_DK), st3),
        pl.BlockSpec((1, ML_HEADS, LANE), st3),
    ]
    out_shape = [jax.ShapeDtypeStruct((bsz, s, ML_V_W), ACT_DTYPE),
                 jax.ShapeDtypeStruct((bsz, ML_HEADS, ML_DV, ML_DK), F32),
                 jax.ShapeDtypeStruct((bsz, ML_HEADS, ML_DK), F32),
                 jax.ShapeDtypeStruct((bsz, ML_HEADS, LANE), F32)]
    out_specs = [pl.BlockSpec((1, T, ML_V_W), lambda b, c: (b, c, 0)),
                 pl.BlockSpec((1, ML_HEADS, ML_DV, ML_DK), lambda b, c: (b, 0, 0, 0)),
                 pl.BlockSpec((1, ML_HEADS, ML_DK), lambda b, c: (b, 0, 0)),
                 pl.BlockSpec((1, ML_HEADS, LANE), lambda b, c: (b, 0, 0))]
    y, cN, nN, mN = pl.pallas_call(
        functools.partial(_mlstm_long_kernel, L=L, CH=CH),
        out_shape=out_shape, grid=(bsz, nc), in_specs=in_specs, out_specs=out_specs,
        scratch_shapes=[pltpu.VMEM((ML_HEADS, ML_DK, ML_ST_W), F32)],
        compiler_params=_cparams(("parallel", "arbitrary")),
        name="mlstm_mixer_long",
    )(u, u, u, u, u, gscan, 0.25 * norm_g.reshape(1, ML_V_W).astype(F32), c0, n0, m0)
    return y, cN, nN, mN[:, :, 0]


LOG2E = 1.4426950408889634


def _swa_kernel(*refs, Lc, G, n_hist, n_invalid):
    sink_ref, q_ref, z_ref, kv_ref = refs[:4]
    pos = 4
    if n_hist:
        hist_ref = refs[pos]
        prev_refs = refs[pos + 1:pos + 1 + n_hist]
        pos += 1 + n_hist
    y_ref = refs[pos]
    c = pl.program_id(1)
    nk = (n_hist + 1) * Lc
    pairs = SW_HEADS // SW_KV // 2
    rows = pairs * Lc

    kv = kv_ref[0]
    if n_hist:
        prev = jnp.concatenate([r[0] for r in prev_refs], axis=0)
        kv = jnp.concatenate([jnp.where(c == 0, hist_ref[0], prev), kv], axis=0)
    kv = kv.astype(F32)
    nkr = kv.shape[0]
    lane = lax.broadcasted_iota(jnp.int32, (nkr, LANE), 1)
    low = lane < SW_HD
    blk = lax.broadcasted_iota(jnp.int32, (rows, 1), 0) // Lc
    ones_v = jnp.ones((nkr, LANE), MXU_DTYPE)

    def split(t, g):
        swapped = pltpu.roll(t, SW_HD, 1)
        if g % 2 == 0:
            lo, hi = t, swapped
        else:
            lo, hi = swapped, t
        return (jnp.where(low, lo, 0.0).astype(MXU_DTYPE), jnp.where(low, 0.0, hi).astype(MXU_DTYPE))

    for g in range(SW_KV):
        kt = kv[:, (g // 2) * LANE:(g // 2 + 1) * LANE] * (SW_HD ** -0.5 * LOG2E)
        vt = kv[:, SW_KV_W + (g // 2) * LANE:SW_KV_W + (g // 2 + 1) * LANE]
        k_par = split(kt, g)
        v_par = tuple(jnp.concatenate([vv, ones_v], axis=1) for vv in split(vt, g))
        sinks = []
        for par in range(2):
            sk = jnp.zeros((rows, 1), F32)
            for j in range(pairs):
                sk = jnp.where(blk == j, sink_ref[SW_HEADS // SW_KV * g + 2 * j + par] * LOG2E, sk)
            sinks.append(sk)

        def scores(i, k_par=k_par):
            ks = slice(i * Lc, i * Lc + nk)
            qs = jnp.concatenate([q_ref[0, i * Lc:(i + 1) * Lc, (pairs * g + j) * LANE:(pairs * g + j + 1) * LANE]
                                  for j in range(pairs)], axis=0)
            ss = []
            for par in range(2):
                s = _nt_dot(qs, k_par[par][ks])
                if n_invalid > i * Lc:
                    key_idx = (c * G + i) * Lc + lax.broadcasted_iota(jnp.int32, (1, nk), 1)
                    s = jnp.where(key_idx >= n_invalid, s, -jnp.inf)
                ss.append(s)
            return ss

        def attend(i, ss, g=g, v_par=v_par, sinks=sinks):
            ks = slice(i * Lc, i * Lc + nk)
            out = jnp.zeros((rows, LANE), F32)
            for par in range(2):
                mx = jnp.maximum(jnp.max(ss[par], axis=1, keepdims=True), sinks[par])
                p = jnp.exp2(ss[par] - mx)
                res = jnp.dot(p.astype(MXU_DTYPE), v_par[par][ks], preferred_element_type=F32)
                den = res[:, LANE:] + jnp.exp2(sinks[par] - mx)
                out = out + res[:, :LANE] / den
            for j in range(pairs):
                cs = slice((pairs * g + j) * LANE, (pairs * g + j + 1) * LANE)
                zg = z_ref[0, i * Lc:(i + 1) * Lc, cs].astype(F32)
                y_ref[0, i * Lc:(i + 1) * Lc, cs] = (out[j * Lc:(j + 1) * Lc] * _silu(zg)).astype(y_ref.dtype)

        pending = None
        for i in range(G + 1):
            nxt = scores(i) if i < G else None
            if pending is not None:
                attend(i - 1, pending)
            pending = nxt


def _swa_mixer(u, hist, sinks, *, Lc, G, n_invalid):
    bsz, s, _ = u.shape
    tq = G * Lc
    nc = s // tq
    kv_blk = 2 * SW_W // (2 * SW_KV_W)
    n_hist = 0 if hist is None else hist.shape[1] // Lc
    in_specs = [pl.BlockSpec(memory_space=pltpu.SMEM),
                pl.BlockSpec((1, tq, SW_W), lambda b, c: (b, c, 0)),
                pl.BlockSpec((1, tq, SW_W), lambda b, c: (b, c, 1)),
                pl.BlockSpec((1, tq, 2 * SW_KV_W), lambda b, c: (b, c, kv_blk))]
    args = [sinks.astype(F32), u, u, u]
    if n_hist:
        hmap = (lambda b, c: (0, 0, 0)) if hist.shape[0] == 1 else (lambda b, c: (b, 0, 0))
        in_specs.append(pl.BlockSpec((1, n_hist * Lc, 2 * SW_KV_W), hmap))
        args.append(hist)
        for i in range(n_hist):
            in_specs.append(pl.BlockSpec(
                (1, Lc, 2 * SW_KV_W),
                lambda b, c, i=i: (b, jnp.maximum(c * G - n_hist + i, 0), kv_blk)))
            args.append(u)
    return pl.pallas_call(
        functools.partial(_swa_kernel, Lc=Lc, G=G, n_hist=n_hist, n_invalid=n_invalid),
        out_shape=jax.ShapeDtypeStruct((bsz, s, SW_W), ACT_DTYPE), grid=(bsz, nc),
        in_specs=in_specs, out_specs=pl.BlockSpec((1, tq, SW_W), lambda b, c: (b, c, 0)),
        compiler_params=_cparams(("parallel", "parallel")),
        name="swa_mixer",
    )(*args)


def _pool_kernel(u_ref, z_ref, h0_ref, wg_ref, sc_ref, y_ref, e_sc, *, T, from_start):
    t = pl.program_id(1)

    @pl.when(t == 0)
    def _():
        e_sc[0:POOL_HIST, :] = h0_ref[0]

    u = u_ref[0].astype(F32)
    e_sc[POOL_HIST:POOL_HIST + T, :] = u
    for g, w in enumerate(POOL_WINDOWS):
        cs = slice(g * POOL_GC, (g + 1) * POOL_GC)
        ug = u[:, cs]
        acc = ug
        for d in range(1, w):
            acc = acc + e_sc[POOL_HIST - d:POOL_HIST - d + T, cs]
        if from_start:
            pos = t * T + lax.broadcasted_iota(jnp.int32, (T, 1), 0) + 1
            pooled = acc / jnp.minimum(pos, w).astype(F32)
        else:
            pooled = acc * (1.0 / w)
        pooled = pooled - ug
        mixed = jnp.dot(pooled.astype(MXU_DTYPE), wg_ref[g], preferred_element_type=F32)
        zg = z_ref[0, :, cs].astype(F32)
        y_ref[0, :, cs] = (mixed * sc_ref[:, cs] * _silu(zg)).astype(y_ref.dtype)
    e_sc[0:POOL_HIST, :] = e_sc[T:T + POOL_HIST, :]


def _pool_mixer(u, hist, w_grp, scale, *, T, from_start):
    bsz, s, w2 = u.shape
    w = w2 // 2
    nt = s // T
    hmap = (lambda b, t: (0, 0, 0)) if hist.shape[0] == 1 else (lambda b, t: (b, 0, 0))
    return pl.pallas_call(
        functools.partial(_pool_kernel, T=T, from_start=from_start),
        out_shape=jax.ShapeDtypeStruct((bsz, s, w), ACT_DTYPE), grid=(bsz, nt),
        in_specs=[pl.BlockSpec((1, T, w), lambda b, t: (b, t, 0)),
                  pl.BlockSpec((1, T, w), lambda b, t: (b, t, 1)),
                  pl.BlockSpec((1, POOL_HIST, w), hmap),
                  pl.BlockSpec((len(POOL_WINDOWS), POOL_GC, POOL_GC), lambda b, t: (0, 0, 0)),
                  pl.BlockSpec((1, w), lambda b, t: (0, 0))],
        out_specs=pl.BlockSpec((1, T, w), lambda b, t: (b, t, 0)),
        scratch_shapes=[pltpu.VMEM((POOL_HIST + T, w), F32)],
        compiler_params=_cparams(("parallel", "arbitrary")),
        name="pool_mixer",
    )(u, u, hist, w_grp, scale.reshape(1, w).astype(F32))


def _pool_band_matrices(T):
    t = np.arange(T)[:, None]
    k = np.arange(T)[None, :]
    th = np.arange(POOL_HIST)[:, None]
    ph = np.arange(POOL_HIST)[None, :] - POOL_HIST
    main = [((k <= t) & (k > t - w)) / w - (k == t) for w in POOL_WINDOWS]
    hist = [(ph > th - w) / w for w in POOL_WINDOWS]
    return np.stack(main).astype(np.float32), np.stack(hist).astype(np.float32)


def _pool_long_kernel(u_ref, z_ref, h0_ref, wg_ref, sc_ref, bm_ref, bh_ref, y_ref, hist_sc, *, T):
    t = pl.program_id(1)

    @pl.when(t == 0)
    def _():
        hist_sc[...] = h0_ref[0]

    for g in range(len(POOL_WINDOWS)):
        cs = slice(g * POOL_GC, (g + 1) * POOL_GC)
        pooled = jnp.dot(bm_ref[g], u_ref[0, :, cs].astype(MXU_DTYPE), preferred_element_type=F32)
        head = pooled[0:POOL_HIST] + jnp.dot(bh_ref[g], hist_sc[:, cs].astype(MXU_DTYPE),
                                             preferred_element_type=F32)
        pooled = jnp.concatenate([head, pooled[POOL_HIST:]], axis=0)
        mixed = jnp.dot(pooled.astype(MXU_DTYPE), wg_ref[g], preferred_element_type=F32)
        zg = z_ref[0, :, cs].astype(F32)
        y_ref[0, :, cs] = (mixed * sc_ref[:, cs] * _silu(zg)).astype(y_ref.dtype)
    hist_sc[...] = u_ref[0, T - POOL_HIST:T, 0:hist_sc.shape[1]]


def _pool_mixer_long(u, hist, w_grp, scale, *, T):
    assert jnp.dtype(ACT_DTYPE).itemsize <= jnp.dtype(MXU_DTYPE).itemsize
    bsz, s, w2 = u.shape
    w = w2 // 2
    nt = s // T
    ng = len(POOL_WINDOWS)
    bm, bh = _pool_band_matrices(T)
    hmap = (lambda b, t: (0, 0, 0)) if hist.shape[0] == 1 else (lambda b, t: (b, 0, 0))
    return pl.pallas_call(
        functools.partial(_pool_long_kernel, T=T),
        out_shape=jax.ShapeDtypeStruct((bsz, s, w), ACT_DTYPE), grid=(bsz, nt),
        in_specs=[pl.BlockSpec((1, T, w), lambda b, t: (b, t, 0)),
                  pl.BlockSpec((1, T, w), lambda b, t: (b, t, 1)),
                  pl.BlockSpec((1, POOL_HIST, w), hmap),
                  pl.BlockSpec((ng, POOL_GC, POOL_GC), lambda b, t: (0, 0, 0)),
                  pl.BlockSpec((1, w), lambda b, t: (0, 0)),
                  pl.BlockSpec((ng, T, T), lambda b, t: (0, 0, 0)),
                  pl.BlockSpec((ng, POOL_HIST, POOL_HIST), lambda b, t: (0, 0, 0))],
        out_specs=pl.BlockSpec((1, T, w), lambda b, t: (b, t, 0)),
        scratch_shapes=[pltpu.VMEM((POOL_HIST, w), ACT_DTYPE)],
        compiler_params=_cparams(("parallel", "arbitrary")),
        name="pool_mixer_long",
    )(u, u, hist, w_grp, scale.reshape(1, w).astype(F32),
      jnp.asarray(bm, MXU_DTYPE), jnp.asarray(bh, MXU_DTYPE))


def _rope_tables(pos):
    half = ROT_DIM // 2
    inv = np.power(ROPE_THETA, -np.arange(half, dtype=np.float64) / half)
    ang = np.asarray(pos, np.float64)[:, None] * inv[None, :]
    cos, sin = np.cos(ang), np.sin(ang)
    pad = np.zeros((ang.shape[0], SW_HD - ROT_DIM))
    ta = np.concatenate([cos, cos, pad + 1.0], axis=1)
    ts = np.concatenate([-sin, sin, pad], axis=1)
    return tuple(jnp.asarray(np.concatenate([t, t], axis=1), F32) for t in (ta, ts))


def _rope_partner_matrix():
    half = ROT_DIM // 2
    p = np.zeros((LANE, LANE), np.float32)
    for d in range(LANE):
        if d % SW_HD < half:
            p[d + half, d] = 1.0
        elif d % SW_HD < ROT_DIM:
            p[d - half, d] = 1.0
    return p


def _pick_tile(m, pref):
    t = pref
    while m % t:
        t //= 2
    return t


def kernel(x_prompt, x_sample, state_mlstm_C, state_mlstm_n, state_mlstm_m, cache_swa_k, cache_swa_v, state_pool, meta_tokens, norm_g, final_norm_g, mlstm_w_in, mlstm_b_if, mlstm_norm_g, mlstm_w_out, swa_w_in, swa_sinks, swa_w_out, pool_w_in, pool_w_grp, pool_scale, pool_w_out):
    bp, sp, d = x_prompt.shape
    bs, ss, _ = x_sample.shape
    depth = norm_g.shape[0]
    mp = bp * sp
    ms_rows = bs * ss
    small = ms_rows + N_META

    xp = x_prompt.reshape(mp, d)
    xs = jnp.concatenate([x_sample.reshape(ms_rows, d), meta_tokens.astype(x_prompt.dtype)], axis=0)
    tm_p = _pick_tile(sp, 1024)
    tm_o = _pick_tile(sp, 512)
    ml_chunk = _pick_tile(sp, 256)
    pool_tile = _pick_tile(sp, 256)

    mlstm_w_t = jnp.swapaxes(mlstm_w_in, 1, 2).astype(MXU_DTYPE)
    pC, pn, pm, pk, pv, pp = [], [], [], [], [], []
    sC, sn, sm, sk, sv, s_pool = [], [], [], [], [], []
    for i in range(depth):
        kind, j = i % N_MIXERS, i // N_MIXERS
        last = i == depth - 1
        fin = final_norm_g if last else None
        if kind == 0:
            gate_w = jnp.pad(mlstm_w_t[j, ML_MAIN_W:], ((0, ML_GATE_PAD - 2 * ML_HEADS), (0, 0)))
            up, gcp, gtp = _proj_in(xp, norm_g[i], mlstm_w_t, tm=tm_p, tn=2048, n=ML_MAIN_W, gate_w=gate_w,
                                    w_transposed=True, w_layer=j)
            us, gcs, gts = _proj_in(xs, norm_g[i], mlstm_w_t, tm=small, tn=1024, n=ML_MAIN_W, gate_w=gate_w,
                                    w_transposed=True, w_layer=j)
            zc = jnp.zeros((1, ML_HEADS, ML_DV, ML_DK), F32)
            zn = jnp.zeros((1, ML_HEADS, ML_DK), F32)
            zm = jnp.zeros((1, ML_HEADS, LANE), F32)
            ym, c_m, n_m, m_m = _mlstm_mixer(
                us[ms_rows:].reshape(1, N_META, -1), gcs[ms_rows:].reshape(1, N_META, -1),
                gts[:, ms_rows:].reshape(1, 1, 2 * ML_HEADS, N_META),
                mlstm_b_if[j], mlstm_norm_g[j], zc, zn, zm, L=N_META)
            ysm, c_s, n_s, m_s = _mlstm_mixer(
                us[:ms_rows].reshape(bs, ss, -1), gcs[:ms_rows].reshape(bs, ss, -1),
                gts[:, :ms_rows].reshape(2 * ML_HEADS, bs, 1, ss).transpose(1, 2, 0, 3),
                mlstm_b_if[j], mlstm_norm_g[j], state_mlstm_C[j].astype(F32), state_mlstm_n[j].astype(F32),
                jnp.broadcast_to(state_mlstm_m[j].astype(F32)[..., None], (bs, ML_HEADS, LANE)), L=ss)
            yp, c_p, n_p, m_p = _mlstm_mixer_long(
                up.reshape(bp, sp, -1), _gate_scan(gtp, mlstm_b_if[j], L=ml_chunk, tb=sp),
                mlstm_norm_g[j], c_m, n_m,
                jnp.broadcast_to(m_m[..., None], (1, ML_HEADS, LANE)), L=ml_chunk)
            pC.append(c_p); pn.append(n_p); pm.append(m_p)
            sC.append(c_s); sn.append(n_s); sm.append(m_s)
            w_out = mlstm_w_out[j]
        elif kind == 1:
            w_in = swa_w_in[j]
            wq, wk, wv, wz = jnp.split(w_in, [SW_W, SW_W + SW_KV_W, SW_W + 2 * SW_KV_W], axis=1)
            w_perm = jnp.concatenate([wq, wz, wk, wv], axis=1).astype(MXU_DTYPE)
            n_in = w_perm.shape[1]
            rope_mask = np.concatenate([np.ones(SW_W, bool), np.zeros(SW_W, bool),
                                        np.ones(SW_KV_W, bool), np.zeros(SW_KV_W, bool)])
            tabs_p = _rope_tables(N_META + np.arange(sp))
            pos_s = np.concatenate([np.tile(N_META + PAST_LEN + np.arange(ss), bs), np.arange(N_META)])
            tabs_s = _rope_tables(pos_s)
            tn_sw = n_in // 3
            up = _proj_in(xp, norm_g[i], w_perm, tm=tm_p, tn=tn_sw, rope_tabs=tabs_p, rope_mask=rope_mask)
            us = _proj_in(xs, norm_g[i], w_perm, tm=small, tn=tn_sw, rope_tabs=tabs_s, rope_mask=rope_mask)
            up = up.reshape(bp, sp, n_in)
            kv_s = us[:ms_rows, 2 * SW_W:].reshape(bs, ss, 2 * SW_KV_W)
            kv_m = us[ms_rows:, 2 * SW_W:].reshape(1, N_META, 2 * SW_KV_W)
            ym = _swa_mixer(us[ms_rows:].reshape(1, N_META, n_in), None, swa_sinks[j],
                            Lc=N_META, G=1, n_invalid=0)
            cache = jnp.concatenate([cache_swa_k[j].reshape(bs, WINDOW, SW_KV_W),
                                     cache_swa_v[j].reshape(bs, WINDOW, SW_KV_W)], axis=-1).astype(ACT_DTYPE)
            ysm = _swa_mixer(us[:ms_rows].reshape(bs, ss, n_in), cache, swa_sinks[j],
                             Lc=SW_CHUNK, G=ss // SW_CHUNK, n_invalid=0)
            hist = jnp.concatenate([jnp.zeros((1, WINDOW - N_META, 2 * SW_KV_W), ACT_DTYPE), kv_m], axis=1)
            yp = _swa_mixer(up, hist, swa_sinks[j], Lc=SW_CHUNK, G=_pick_tile(sp // SW_CHUNK, 8),
                            n_invalid=WINDOW - N_META)
            kv_p = up[:, -WINDOW:, 2 * SW_W:]
            pk.append(kv_p[:, :, :SW_KV_W].astype(F32).reshape(bp, WINDOW, SW_KV, SW_HD))
            pv.append(kv_p[:, :, SW_KV_W:].astype(F32).reshape(bp, WINDOW, SW_KV, SW_HD))
            k_new = kv_s[:, :, :SW_KV_W].astype(F32).reshape(bs, ss, SW_KV, SW_HD)
            v_new = kv_s[:, :, SW_KV_W:].astype(F32).reshape(bs, ss, SW_KV, SW_HD)
            sk.append(jnp.concatenate([cache_swa_k[j].astype(F32), k_new], axis=1)[:, -WINDOW:])
            sv.append(jnp.concatenate([cache_swa_v[j].astype(F32), v_new], axis=1)[:, -WINDOW:])
            w_out = swa_w_out[j]
        else:
            w_in = pool_w_in[j].astype(MXU_DTYPE)
            pw = w_in.shape[1] // 2
            w_grp = pool_w_grp[j].astype(MXU_DTYPE)
            up = _proj_in(xp, norm_g[i], w_in, tm=tm_p, tn=2048).reshape(bp, sp, 2 * pw)
            us = _proj_in(xs, norm_g[i], w_in, tm=small, tn=1024)
            u_m = us[ms_rows:].reshape(1, N_META, 2 * pw)
            u_s = us[:ms_rows].reshape(bs, ss, 2 * pw)
            ym = _pool_mixer(u_m, jnp.zeros((1, POOL_HIST, pw), F32), w_grp, pool_scale[j],
                             T=N_META, from_start=True)
            hist_s = jnp.pad(state_pool[j].astype(F32), ((0, 0), (POOL_HIST - state_pool.shape[2], 0), (0, 0)))
            ysm = _pool_mixer(u_s, hist_s, w_grp, pool_scale[j], T=ss, from_start=False)
            yp = _pool_mixer_long(up, u_m[:, :, :pw], w_grp, pool_scale[j], T=pool_tile)
            n_keep = state_pool.shape[2]
            pp.append(up[:, -n_keep:, :pw].astype(F32))
            s_pool.append(u_s[:, -n_keep:, :pw].astype(F32))
            w_out = pool_w_out[j]
        w_out = w_out.astype(MXU_DTYPE)
        y_small = jnp.concatenate([ysm.reshape(ms_rows, -1), ym.reshape(N_META, -1)], axis=0)
        xp = _proj_out(xp, yp.reshape(mp, -1), w_out, tm=tm_o, final_g=fin)
        xs = _proj_out(xs, y_small, w_out, tm=small, final_g=fin)

    y_prompt = xp.reshape(bp, sp, d)
    y_sample = xs[:ms_rows].reshape(bs, ss, d)
    return (y_prompt, y_sample,
            jnp.stack(pC), jnp.stack(pn), jnp.stack(pm), jnp.stack(pk), jnp.stack(pv), jnp.stack(pp),
            jnp.stack(sC), jnp.stack(sn), jnp.stack(sm), jnp.stack(sk), jnp.stack(sv), jnp.stack(s_pool))
```

```python
import functools

import numpy as np
import jax
import jax.numpy as jnp
from jax import lax
from jax.experimental import pallas as pl
from jax.experimental.pallas import tpu as pltpu

F32 = jnp.float32
MXU_DTYPE = jnp.bfloat16
ACT_DTYPE = jnp.bfloat16

EPS = 1e-6
N_META = 16
N_MIXERS = 3
PAST_LEN = 1024

ML_HEADS = 8
ML_DK = 128
ML_DV = 256
ML_QK_W = ML_HEADS * ML_DK
ML_V_W = ML_HEADS * ML_DV
ML_MAIN_W = 2 * ML_QK_W + 3 * ML_V_W
ML_GATE_PAD = 128

SW_HEADS = 32
SW_KV = 4
SW_HD = 64
SW_W = SW_HEADS * SW_HD
SW_KV_W = SW_KV * SW_HD
WINDOW = 128
SW_CHUNK = 64
ROT_DIM = 16
ROPE_THETA = 500000.0

POOL_WINDOWS = (2, 4, 8, 16)
POOL_GC = 512
POOL_HIST = 16

LANE = 128
VMEM_LIMIT_BYTES = 56 * 1024 * 1024


def _cparams(sem):
    return pltpu.CompilerParams(dimension_semantics=sem, vmem_limit_bytes=VMEM_LIMIT_BYTES)


def _nt_dot(a, b):
    return lax.dot_general(a, b, (((1,), (1,)), ((), ())), preferred_element_type=F32)


def _tn_dot(a, b):
    return lax.dot_general(a, b, (((0,), (0,)), ((), ())), preferred_element_type=F32)


def _sigmoid(x):
    return 0.5 + 0.5 * jnp.tanh(0.5 * x)


def _silu(x):
    return x * _sigmoid(x)


def _log_sigmoid(x):
    return jnp.minimum(x, 0.0) - jnp.log1p(jnp.exp(-jnp.abs(x)))


def _proj_in_kernel(*refs, rope, gates, w_transposed):
    x_ref, g_ref, w_ref = refs[:3]
    pos = 3
    if rope:
        ta_ref, ts_ref, perm_ref = refs[pos:pos + 3]
        pos += 3
    if gates:
        wg_ref = refs[pos]
        pos += 1
    o_ref = refs[pos]
    pos += 1
    if gates:
        gc_ref, gt_ref = refs[pos:pos + 2]
        pos += 2
    hn_sc = refs[pos]
    j = pl.program_id(1)

    @pl.when(j == 0)
    def _():
        x = x_ref[...]
        ms = jnp.mean(x * x, axis=1, keepdims=True)
        hn = (x * lax.rsqrt(ms + EPS) * g_ref[...]).astype(MXU_DTYPE)
        hn_sc[...] = hn
        if gates:
            gcol = _nt_dot(hn, wg_ref[...])
            gc_ref[...] = gcol[:, :2 * ML_HEADS]
            gt_ref[...] = gcol.T[:2 * ML_HEADS, :]

    if w_transposed:
        acc = _nt_dot(hn_sc[...], w_ref[...])
    else:
        acc = jnp.dot(hn_sc[...], w_ref[...], preferred_element_type=F32)
    if not rope:
        o_ref[...] = acc.astype(o_ref.dtype)
        return

    def store(pattern):
        pieces = []
        for l, rotate in enumerate(pattern):
            a = acc[:, l * LANE:(l + 1) * LANE]
            if rotate:
                partner = jnp.dot(a.astype(MXU_DTYPE), perm_ref[...], preferred_element_type=F32)
                a = a * ta_ref[...] + partner * ts_ref[...]
            pieces.append(a.astype(o_ref.dtype))
        o_ref[...] = jnp.concatenate(pieces, axis=1)

    for pattern in sorted(set(rope)):
        tiles = [t for t, p in enumerate(rope) if p == pattern]
        cond = functools.reduce(jnp.logical_or, [j == t for t in tiles])
        pl.when(cond)(functools.partial(store, pattern))


def _proj_in(x, g, w, *, tm, tn, n=None, rope_tabs=None, rope_mask=None, gate_w=None, w_transposed=False,
             w_layer=None):
    m, d = x.shape
    if n is None:
        n = w.shape[0] if w_transposed else w.shape[1]
    rope = ()
    if rope_tabs is not None:
        groups = rope_mask.reshape(n // LANE, LANE)
        assert (groups == groups[:, :1]).all()
        rope = tuple(tuple(bool(f) for f in groups[t * tn // LANE:(t + 1) * tn // LANE, 0])
                     for t in range(n // tn))
    gates = gate_w is not None
    grid = (m // tm, n // tn)
    in_specs = [pl.BlockSpec((tm, d), lambda i, j: (i, 0)),
                pl.BlockSpec((1, d), lambda i, j: (0, 0)),
                pl.BlockSpec((None, tn, d), lambda i, j: (w_layer, j, 0)) if w_layer is not None
                else pl.BlockSpec((tn, d), lambda i, j: (j, 0)) if w_transposed
                else pl.BlockSpec((d, tn), lambda i, j: (0, j))]
    args = [x, g.reshape(1, d), w]
    if rope:
        nrep = rope_tabs[0].shape[0] // tm
        for t in rope_tabs:
            in_specs.append(pl.BlockSpec((tm, LANE), lambda i, j, nrep=nrep: (i % nrep, 0)))
            args.append(t)
        in_specs.append(pl.BlockSpec((LANE, LANE), lambda i, j: (0, 0)))
        args.append(jnp.asarray(_rope_partner_matrix(), MXU_DTYPE))
    if gates:
        in_specs.append(pl.BlockSpec((ML_GATE_PAD, d), lambda i, j: (0, 0)))
        args.append(gate_w)
    out_shape = [jax.ShapeDtypeStruct((m, n), ACT_DTYPE)]
    out_specs = [pl.BlockSpec((tm, tn), lambda i, j: (i, j))]
    if gates:
        out_shape += [jax.ShapeDtypeStruct((m, 2 * ML_HEADS), F32),
                      jax.ShapeDtypeStruct((2 * ML_HEADS, m), F32)]
        out_specs += [pl.BlockSpec((tm, 2 * ML_HEADS), lambda i, j: (i, 0)),
                      pl.BlockSpec((2 * ML_HEADS, tm), lambda i, j: (0, i))]
    res = pl.pallas_call(
        functools.partial(_proj_in_kernel, rope=rope, gates=gates, w_transposed=w_transposed),
        out_shape=out_shape, grid=grid, in_specs=in_specs, out_specs=out_specs,
        scratch_shapes=[pltpu.VMEM((tm, d), MXU_DTYPE)],
        compiler_params=_cparams(("parallel", "arbitrary")),
        name="proj_in",
    )(*args)
    return res if gates else res[0]


def _proj_out_kernel(*refs, final_norm):
    y_ref, w_ref, x_ref = refs[:3]
    if final_norm:
        g_ref, o_ref = refs[3:5]
    else:
        o_ref = refs[3]
    acc = x_ref[...] + jnp.dot(y_ref[...].astype(MXU_DTYPE), w_ref[...], preferred_element_type=F32)
    if final_norm:
        ms = jnp.mean(acc * acc, axis=1, keepdims=True)
        acc = acc * lax.rsqrt(ms + EPS) * g_ref[...]
    o_ref[...] = acc


def _proj_out(x, y, w, *, tm, final_g=None):
    m, d = x.shape
    k = y.shape[1]
    final_norm = final_g is not None
    in_specs = [pl.BlockSpec((tm, k), lambda i: (i, 0)),
                pl.BlockSpec((k, d), lambda i: (0, 0)),
                pl.BlockSpec((tm, d), lambda i: (i, 0))]
    args = [y, w, x]
    if final_norm:
        in_specs.append(pl.BlockSpec((1, d), lambda i: (0, 0)))
        args.append(final_g.reshape(1, d))
    return pl.pallas_call(
        functools.partial(_proj_out_kernel, final_norm=final_norm),
        out_shape=jax.ShapeDtypeStruct((m, d), F32), grid=(m // tm,),
        in_specs=in_specs, out_specs=pl.BlockSpec((tm, d), lambda i: (i, 0)),
        compiler_params=_cparams(("parallel",)),
        name="proj_out",
    )(*args)


def _mlstm_kernel(q_ref, k_ref, v_ref, o_ref, z_ref, gc_ref, gt_ref, bc_ref, br_ref, ng_ref,
                  c0_ref, n0_ref, m0_ref, y_ref, cs_ref, ns_ref, ms_ref, *, L):
    c = pl.program_id(1)

    @pl.when(c == 0)
    def _():
        cs_ref[...] = c0_ref[...]
        ns_ref[...] = n0_ref[...]
        ms_ref[...] = m0_ref[...]

    row = lax.broadcasted_iota(jnp.int32, (L, L), 0)
    col = lax.broadcasted_iota(jnp.int32, (L, L), 1)
    tril = col <= row
    scale = ML_DK ** -0.5

    gc = gc_ref[0] + bc_ref[...]
    lane16 = lax.broadcasted_iota(jnp.int32, gc.shape, 1)
    gc = jnp.where(lane16 < ML_HEADS, gc, _log_sigmoid(gc))
    gt = gt_ref[0, 0] + br_ref[...]
    sub16 = lax.broadcasted_iota(jnp.int32, gt.shape, 0)
    gt = jnp.where(sub16 < ML_HEADS, gt, _log_sigmoid(gt))

    for h in range(ML_HEADS):
        li_col = gc[:, h:h + 1]
        lf_col = gc[:, ML_HEADS + h:ML_HEADS + h + 1]
        li_row = gt[h:h + 1, :]
        lf_row = gt[ML_HEADS + h:ML_HEADS + h + 1, :]
        q = q_ref[0, :, h * ML_DK:(h + 1) * ML_DK]
        k = k_ref[0, :, h * ML_DK:(h + 1) * ML_DK]
        v = v_ref[0, :, h * ML_DV:(h + 1) * ML_DV]
        c_old = cs_ref[0, h]
        n_old = ns_ref[0, h:h + 1, :]
        m_old = ms_ref[0, h:h + 1, 0:1]

        b_col = jnp.sum(jnp.where(tril, lf_row, 0.0), axis=1, keepdims=True)
        b_row = jnp.sum(jnp.where(row <= col, lf_col, 0.0), axis=0, keepdims=True)
        dmat = jnp.where(tril, b_col - b_row + li_row, -jnp.inf)
        inter = b_col + m_old
        m_t = jnp.maximum(inter, jnp.max(dmat, axis=1, keepdims=True))
        p = jnp.exp(dmat - m_t)
        w_inter = jnp.exp(inter - m_t)
        s = p * (_nt_dot(q, k) * scale)
        qf = q.astype(F32)
        kf = k.astype(F32)
        num = w_inter * _nt_dot(q, c_old.astype(MXU_DTYPE)) + jnp.dot(
            s.astype(MXU_DTYPE), v, preferred_element_type=F32)
        den = w_inter * jnp.sum(qf * n_old, axis=1, keepdims=True) + jnp.sum(s, axis=1, keepdims=True)
        hh = num / jnp.maximum(jnp.abs(den), jnp.exp(-m_t))

        hn = hh * lax.rsqrt(jnp.mean(hh * hh, axis=1, keepdims=True) + EPS)
        hn = hn * ng_ref[:, h * ML_DV:(h + 1) * ML_DV]
        og = o_ref[0, :, h * ML_DV:(h + 1) * ML_DV].astype(F32)
        zg = z_ref[0, :, h * ML_DV:(h + 1) * ML_DV].astype(F32)
        y_ref[0, :, h * ML_DV:(h + 1) * ML_DV] = (hn * _sigmoid(og) * _silu(zg)).astype(y_ref.dtype)

        b_end = jnp.sum(lf_row, axis=1, keepdims=True)
        a_end = b_end - b_col + li_col
        m_new = jnp.maximum(b_end + m_old, jnp.max(a_end, axis=0, keepdims=True))
        w_old = jnp.exp(b_end + m_old - m_new)
        w_s = jnp.exp(a_end - m_new)
        vw = (v.astype(F32) * w_s).astype(MXU_DTYPE)
        cs_ref[0, h] = w_old * c_old + _tn_dot(vw, k) * scale
        ns_ref[0, h:h + 1, :] = w_old * n_old + jnp.sum(kf * w_s, axis=0, keepdims=True) * scale
        ms_ref[0, h:h + 1, :] = jnp.broadcast_to(m_new, (1, LANE))


def _mlstm_mixer(u, gc, gt, b_if, norm_g, c0, n0, m0, *, L):
    bsz, s, _ = u.shape
    nc = s // L
    shared = c0.shape[0] == 1
    st = (lambda b, c: (0, 0, 0, 0)) if shared else (lambda b, c: (b, 0, 0, 0))
    st3 = (lambda b, c: (0, 0, 0)) if shared else (lambda b, c: (b, 0, 0))
    qk_b = ML_QK_W
    v_b = ML_V_W
    in_specs = [
        pl.BlockSpec((1, L, qk_b), lambda b, c: (b, c, 0)),
        pl.BlockSpec((1, L, qk_b), lambda b, c: (b, c, 1)),
        pl.BlockSpec((1, L, v_b), lambda b, c: (b, c, 1)),
        pl.BlockSpec((1, L, v_b), lambda b, c: (b, c, 2)),
        pl.BlockSpec((1, L, v_b), lambda b, c: (b, c, 3)),
        pl.BlockSpec((1, L, 2 * ML_HEADS), lambda b, c: (b, c, 0)),
        pl.BlockSpec((1, 1, 2 * ML_HEADS, L), lambda b, c: (b, c, 0, 0)),
        pl.BlockSpec((1, 2 * ML_HEADS), lambda b, c: (0, 0)),
        pl.BlockSpec((2 * ML_HEADS, 1), lambda b, c: (0, 0)),
        pl.BlockSpec((1, ML_V_W), lambda b, c: (0, 0)),
        pl.BlockSpec((1, ML_HEADS, ML_DV, ML_DK), st),
        pl.BlockSpec((1, ML_HEADS, ML_DK), st3),
        pl.BlockSpec((1, ML_HEADS, LANE), st3),
    ]
    out_shape = [jax.ShapeDtypeStruct((bsz, s, ML_V_W), ACT_DTYPE),
                 jax.ShapeDtypeStruct((bsz, ML_HEADS, ML_DV, ML_DK), F32),
                 jax.ShapeDtypeStruct((bsz, ML_HEADS, ML_DK), F32),
                 jax.ShapeDtypeStruct((bsz, ML_HEADS, LANE), F32)]
    out_specs = [pl.BlockSpec((1, L, ML_V_W), lambda b, c: (b, c, 0)),
                 pl.BlockSpec((1, ML_HEADS, ML_DV, ML_DK), lambda b, c: (b, 0, 0, 0)),
                 pl.BlockSpec((1, ML_HEADS, ML_DK), lambda b, c: (b, 0, 0)),
                 pl.BlockSpec((1, ML_HEADS, LANE), lambda b, c: (b, 0, 0))]
    b_flat = b_if.reshape(2 * ML_HEADS).astype(F32)
    y, cN, nN, mN = pl.pallas_call(
        functools.partial(_mlstm_kernel, L=L),
        out_shape=out_shape, grid=(bsz, nc), in_specs=in_specs, out_specs=out_specs,
        compiler_params=_cparams(("parallel", "arbitrary")),
        name="mlstm_mixer",
    )(u, u, u, u, u, gc, gt, b_flat.reshape(1, -1), b_flat.reshape(-1, 1),
      norm_g.reshape(1, ML_V_W).astype(F32), c0, n0, m0)
    return y, cN, nN, mN[:, :, 0]


ML_ST_W = ML_DV + LANE
ML_EXT_ROWS = 16


def _split3(x):
    x1 = x.astype(jnp.bfloat16).astype(F32)
    r1 = x - x1
    x2 = r1.astype(jnp.bfloat16).astype(F32)
    x3 = (r1 - x2).astype(jnp.bfloat16).astype(F32)
    return x1, x2, x3


def _gate_scan_kernel(g_ref, br_ref, o_ref, *, L):
    H = ML_HEADS
    g = g_ref[...] + br_ref[...]
    li = g[0:H]
    pos = lax.rem(lax.broadcasted_iota(jnp.int32, li.shape, 1), L)
    b = _log_sigmoid(g[H:2 * H])
    d = 1
    while d < L:
        b = b + jnp.where(pos >= d, pltpu.roll(b, d, 1), 0.0)
        d *= 2
    beta0 = li - b
    cm = beta0
    d = 1
    while d < L:
        cm = jnp.maximum(cm, jnp.where(pos >= d, pltpu.roll(cm, d, 1), -jnp.inf))
        d *= 2
    o_ref[0:H, :] = beta0
    o_ref[H:2 * H, :] = b
    o_ref[2 * H:3 * H, :] = cm


def _gate_scan(gt, b_if, *, L, tb):
    m = gt.shape[1]
    return pl.pallas_call(
        functools.partial(_gate_scan_kernel, L=L),
        out_shape=jax.ShapeDtypeStruct((3 * ML_HEADS, m), F32), grid=(m // tb,),
        in_specs=[pl.BlockSpec((2 * ML_HEADS, tb), lambda i: (0, i)),
                  pl.BlockSpec((2 * ML_HEADS, 1), lambda i: (0, 0))],
        out_specs=pl.BlockSpec((3 * ML_HEADS, tb), lambda i: (0, i)),
        compiler_params=_cparams(("parallel",)),
        name="mlstm_gate_scan",
    )(gt, b_if.reshape(2 * ML_HEADS, 1).astype(F32))


def _mlstm_long_kernel(q_ref, k_ref, v_ref, o_ref, z_ref, gs_ref, ng_ref, c0_ref, n0_ref, m0_ref,
                       y_ref, cs_ref, ns_ref, ms_ref, st_sc, *, L, CH):
    c = pl.program_id(1)
    nc = pl.num_programs(1)
    H = ML_HEADS
    bf = jnp.bfloat16

    @pl.when(c == 0)
    def _():
        for h in range(H):
            st_sc[h, :, 0:ML_DV] = c0_ref[0, h].T
            st_sc[h, :, ML_DV:ML_ST_W] = jnp.broadcast_to(n0_ref[0, h:h + 1, :], (ML_DK, LANE)).T
        ms_ref[...] = m0_ref[...]

    row = lax.broadcasted_iota(jnp.int32, (L, L), 0)
    col = lax.broadcasted_iota(jnp.int32, (L, L), 1)
    tril = col <= row

    ones_l = jnp.ones((3, L), F32)
    ones_s = jnp.ones((3, LANE), F32)
    sub = lax.broadcasted_iota(jnp.int32, (ML_EXT_ROWS, LANE), 0)
    rhs_f = jnp.where((sub < 3) | ((sub >= 6) & (sub < 9)), 1.0, 0.0)
    ones_v = jnp.ones((L, LANE), MXU_DTYPE)
    mean_w = jnp.full((ML_DV, LANE), 1.0 / ML_DV, MXU_DTYPE)

    def chunk_scalars(cc):
        ls = slice(cc * L, (cc + 1) * L)
        beta0 = gs_ref[0:H, ls]
        b = gs_ref[H:2 * H, ls]
        cm = gs_ref[2 * H:3 * H, ls]
        m_old = ms_ref[0]
        mm = jnp.maximum(jnp.concatenate([m_old] * (L // LANE), axis=1), cm)
        return (_split3(-mm * LOG2E), _split3((beta0 + jnp.log(jnp.float32(ML_DK ** -0.5))) * LOG2E),
                _split3(-b * LOG2E), _split3(m_old * LOG2E))

    def front(cc, h, scal):
        a3, b3, n3, o3 = scal
        hs = slice(h, h + 1)
        rs_ = slice(cc * L, (cc + 1) * L)
        lhs = jnp.concatenate([a3[0][hs], a3[1][hs], a3[2][hs], ones_l, n3[0][hs], n3[1][hs], n3[2][hs],
                               jnp.zeros((ML_EXT_ROWS - 9, L), F32)], axis=0)
        rhs_e = jnp.concatenate([ones_l, b3[0][hs], b3[1][hs], b3[2][hs],
                                 jnp.zeros((ML_EXT_ROWS - 6, L), F32)], axis=0)
        rhs_g = jnp.concatenate([ones_s, o3[0][hs], o3[1][hs], o3[2][hs],
                                 jnp.zeros((ML_EXT_ROWS - 6, LANE), F32)], axis=0)
        rhs = jnp.concatenate([rhs_g, rhs_f, rhs_e], axis=1).astype(bf)
        ext = _tn_dot(lhs.astype(bf), rhs)
        ef = ext[:, LANE:2 * LANE]
        wg = jnp.exp2(ext[:, 0:LANE])
        fl = jnp.exp2(ef)
        p = jnp.exp2(jnp.where(tril, ext[:, 2 * LANE:], -jnp.inf))

        q = q_ref[0, rs_, h * ML_DK:(h + 1) * ML_DK]
        k = k_ref[0, rs_, h * ML_DK:(h + 1) * ML_DK]
        v = v_ref[0, rs_, h * ML_DV:(h + 1) * ML_DV]
        s = p * _nt_dot(q, k)
        qw = q.astype(F32) * wg
        x = jnp.concatenate([qw.astype(MXU_DTYPE), s.astype(MXU_DTYPE)], axis=1)
        vo = jnp.concatenate([v.astype(MXU_DTYPE), ones_v], axis=1)
        st = st_sc[h]
        res = jnp.dot(x, jnp.concatenate([st.astype(MXU_DTYPE), vo], axis=0), preferred_element_type=F32)

        ktw = (k.astype(F32).T * p[L - 1:L, :]).astype(MXU_DTYPE)
        w_old = wg[L - 1:L, :]
        st_sc[h] = st * jnp.concatenate([w_old] * (ML_ST_W // LANE), axis=1) + jnp.dot(
            ktw, vo, preferred_element_type=F32)
        ms_ref[0, hs, :] = ef[L - 1:L, :] * (-1.0 / LOG2E)
        return res, fl

    def back(cc, h, res, fl):
        rs_ = slice(cc * L, (cc + 1) * L)
        r = 1.0 / jnp.maximum(jnp.abs(res[:, ML_DV:]), fl)
        hh = res[:, :ML_DV] * jnp.concatenate([r, r], axis=1)
        msq = jnp.dot((hh * hh).astype(MXU_DTYPE), mean_w, preferred_element_type=F32)
        rs = lax.rsqrt(msq + EPS)
        hn = hh * jnp.concatenate([rs, rs], axis=1) * ng_ref[:, h * ML_DV:(h + 1) * ML_DV]
        og = o_ref[0, rs_, h * ML_DV:(h + 1) * ML_DV].astype(F32)
        zg = z_ref[0, rs_, h * ML_DV:(h + 1) * ML_DV].astype(F32)
        gate = zg * (1.0 + jnp.tanh(0.5 * og)) * (1.0 + jnp.tanh(0.5 * zg))
        y_ref[0, rs_, h * ML_DV:(h + 1) * ML_DV] = (hn * gate).astype(y_ref.dtype)

    items = [(cc, h) for cc in range(CH) for h in range(H)]
    pending = None
    scal = None
    for n_item in range(len(items) + 1):
        nxt = None
        if n_item < len(items):
            cc, h = items[n_item]
            if h == 0:
                scal = chunk_scalars(cc)
            nxt = front(cc, h, scal)
        if pending is not None:
            back(*items[n_item - 1], *pending)
        pending = nxt

    @pl.when(c == nc - 1)
    def _():
        for h in range(H):
            cs_ref[0, h] = st_sc[h, :, 0:ML_DV].T
            ns_ref[0, h:h + 1, :] = st_sc[h, :, ML_DV:ML_ST_W].T[0:1, :]


def _mlstm_mixer_long(u, gscan, norm_g, c0, n0, m0, *, L):
    bsz, s, _ = u.shape
    CH = _pick_tile(s // L, 4)
    T = CH * L
    nc = s // T
    shared = c0.shape[0] == 1
    st = (lambda b, c: (0, 0, 0, 0)) if shared else (lambda b, c: (b, 0, 0, 0))
    st3 = (lambda b, c: (0, 0, 0)) if shared else (lambda b, c: (b, 0, 0))
    in_specs = [
        pl.BlockSpec((1, T, ML_QK_W), lambda b, c: (b, c, 0)),
        pl.BlockSpec((1, T, ML_QK_W), lambda b, c: (b, c, 1)),
        pl.BlockSpec((1, T, ML_V_W), lambda b, c: (b, c, 1)),
        pl.BlockSpec((1, T, ML_V_W), lambda b, c: (b, c, 2)),
        pl.BlockSpec((1, T, ML_V_W), lambda b, c: (b, c, 3)),
        pl.BlockSpec((3 * ML_HEADS, T), lambda b, c, nc=nc: (0, b * nc + c)),
        pl.BlockSpec((1, ML_V_W), lambda b, c: (0, 0)),
        pl.BlockSpec((1, ML_HEADS, ML_DV, ML_DK), st),
        pl.BlockSpec((1, ML_HEADS, ML_DK), st3),
        pl.BlockSpec((1, ML_HEADS, LANE), st3),
    ]
    out_shape = [jax.ShapeDtypeStruct((bsz, s, ML_V_W), ACT_DTYPE),
                 jax.ShapeDtypeStruct((bsz, ML_HEADS, ML_DV, ML_DK), F32),
                 jax.ShapeDtypeStruct((bsz, ML_HEADS, ML_DK), F32),
                 jax.ShapeDtypeStruct((bsz, ML_HEADS, LANE), F32)]
    out_specs = [pl.BlockSpec((1, T, ML_V_W), lambda b, c: (b, c, 0)),
                 pl.BlockSpec((1, ML_HEADS, ML_DV, ML_DK), lambda b, c: (b, 0, 0, 0)),
                 pl.BlockSpec((1, ML_HEADS, ML_DK), lambda b, c: (b, 0, 0)),
                 pl.BlockSpec((1, ML_HEADS, LANE), lambda b, c: (b, 0, 0))]
    y, cN, nN, mN = pl.pallas_call(
        functools.partial(_mlstm_long_kernel, L=L, CH=CH),
        out_shape=out_shape, grid=(bsz, nc), in_specs=in_specs, out_specs=out_specs,
        scratch_shapes=[pltpu.VMEM((ML_HEADS, ML_DK, ML_ST_W), F32)],
        compiler_params=_cparams(("parallel", "arbitrary")),
        name="mlstm_mixer_long",
    )(u, u, u, u, u, gscan, 0.25 * norm_g.reshape(1, ML_V_W).astype(F32), c0, n0, m0)
    return y, cN, nN, mN[:, :, 0]


LOG2E = 1.4426950408889634


def _swa_kernel(*refs, Lc, G, n_hist, n_invalid):
    sink_ref, q_ref, z_ref, kv_ref = refs[:4]
    pos = 4
    if n_hist:
        hist_ref = refs[pos]
        prev_refs = refs[pos + 1:pos + 1 + n_hist]
        pos += 1 + n_hist
    y_ref = refs[pos]
    c = pl.program_id(1)
    nk = (n_hist + 1) * Lc
    pairs = SW_HEADS // SW_KV // 2
    rows = pairs * Lc

    kv = kv_ref[0]
    if n_hist:
        prev = jnp.concatenate([r[0] for r in prev_refs], axis=0)
        kv = jnp.concatenate([jnp.where(c == 0, hist_ref[0], prev), kv], axis=0)
    kv = kv.astype(F32)
    nkr = kv.shape[0]
    lane = lax.broadcasted_iota(jnp.int32, (nkr, LANE), 1)
    low = lane < SW_HD
    blk = lax.broadcasted_iota(jnp.int32, (rows, 1), 0) // Lc
    ones_v = jnp.ones((nkr, LANE), MXU_DTYPE)

    def split(t, g):
        swapped = pltpu.roll(t, SW_HD, 1)
        if g % 2 == 0:
            lo, hi = t, swapped
        else:
            lo, hi = swapped, t
        return (jnp.where(low, lo, 0.0).astype(MXU_DTYPE), jnp.where(low, 0.0, hi).astype(MXU_DTYPE))

    for g in range(SW_KV):
        kt = kv[:, (g // 2) * LANE:(g // 2 + 1) * LANE] * (SW_HD ** -0.5 * LOG2E)
        vt = kv[:, SW_KV_W + (g // 2) * LANE:SW_KV_W + (g // 2 + 1) * LANE]
        k_par = split(kt, g)
        v_par = tuple(jnp.concatenate([vv, ones_v], axis=1) for vv in split(vt, g))
        sinks = []
        for par in range(2):
            sk = jnp.zeros((rows, 1), F32)
            for j in range(pairs):
                sk = jnp.where(blk == j, sink_ref[SW_HEADS // SW_KV * g + 2 * j + par] * LOG2E, sk)
            sinks.append(sk)

        def scores(i, k_par=k_par):
            ks = slice(i * Lc, i * Lc + nk)
            qs = jnp.concatenate([q_ref[0, i * Lc:(i + 1) * Lc, (pairs * g + j) * LANE:(pairs * g + j + 1) * LANE]
                                  for j in range(pairs)], axis=0)
            ss = []
            for par in range(2):
                s = _nt_dot(qs, k_par[par][ks])
                if n_invalid > i * Lc:
                    key_idx = (c * G + i) * Lc + lax.broadcasted_iota(jnp.int32, (1, nk), 1)
                    s = jnp.where(key_idx >= n_invalid, s, -jnp.inf)
                ss.append(s)
            return ss

        def attend(i, ss, g=g, v_par=v_par, sinks=sinks):
            ks = slice(i * Lc, i * Lc + nk)
            out = jnp.zeros((rows, LANE), F32)
            for par in range(2):
                mx = jnp.maximum(jnp.max(ss[par], axis=1, keepdims=True), sinks[par])
                p = jnp.exp2(ss[par] - mx)
                res = jnp.dot(p.astype(MXU_DTYPE), v_par[par][ks], preferred_element_type=F32)
                den = res[:, LANE:] + jnp.exp2(sinks[par] - mx)
                out = out + res[:, :LANE] / den
            for j in range(pairs):
                cs = slice((pairs * g + j) * LANE, (pairs * g + j + 1) * LANE)
                zg = z_ref[0, i * Lc:(i + 1) * Lc, cs].astype(F32)
                y_ref[0, i * Lc:(i + 1) * Lc, cs] = (out[j * Lc:(j + 1) * Lc] * _silu(zg)).astype(y_ref.dtype)

        pending = None
        for i in range(G + 1):
            nxt = scores(i) if i < G else None
            if pending is not None:
                attend(i - 1, pending)
            pending = nxt


def _swa_mixer(u, hist, sinks, *, Lc, G, n_invalid):
    bsz, s, _ = u.shape
    tq = G * Lc
    nc = s // tq
    kv_blk = 2 * SW_W // (2 * SW_KV_W)
    n_hist = 0 if hist is None else hist.shape[1] // Lc
    in_specs = [pl.BlockSpec(memory_space=pltpu.SMEM),
                pl.BlockSpec((1, tq, SW_W), lambda b, c: (b, c, 0)),
                pl.BlockSpec((1, tq, SW_W), lambda b, c: (b, c, 1)),
                pl.BlockSpec((1, tq, 2 * SW_KV_W), lambda b, c: (b, c, kv_blk))]
    args = [sinks.astype(F32), u, u, u]
    if n_hist:
        hmap = (lambda b, c: (0, 0, 0)) if hist.shape[0] == 1 else (lambda b, c: (b, 0, 0))
        in_specs.append(pl.BlockSpec((1, n_hist * Lc, 2 * SW_KV_W), hmap))
        args.append(hist)
        for i in range(n_hist):
            in_specs.append(pl.BlockSpec(
                (1, Lc, 2 * SW_KV_W),
                lambda b, c, i=i: (b, jnp.maximum(c * G - n_hist + i, 0), kv_blk)))
            args.append(u)
    return pl.pallas_call(
        functools.partial(_swa_kernel, Lc=Lc, G=G, n_hist=n_hist, n_invalid=n_invalid),
        out_shape=jax.ShapeDtypeStruct((bsz, s, SW_W), ACT_DTYPE), grid=(bsz, nc),
        in_specs=in_specs, out_specs=pl.BlockSpec((1, tq, SW_W), lambda b, c: (b, c, 0)),
        compiler_params=_cparams(("parallel", "parallel")),
        name="swa_mixer",
    )(*args)


def _pool_kernel(u_ref, z_ref, h0_ref, wg_ref, sc_ref, y_ref, e_sc, *, T, from_start):
    t = pl.program_id(1)

    @pl.when(t == 0)
    def _():
        e_sc[0:POOL_HIST, :] = h0_ref[0]

    u = u_ref[0].astype(F32)
    e_sc[POOL_HIST:POOL_HIST + T, :] = u
    for g, w in enumerate(POOL_WINDOWS):
        cs = slice(g * POOL_GC, (g + 1) * POOL_GC)
        ug = u[:, cs]
        acc = ug
        for d in range(1, w):
            acc = acc + e_sc[POOL_HIST - d:POOL_HIST - d + T, cs]
        if from_start:
            pos = t * T + lax.broadcasted_iota(jnp.int32, (T, 1), 0) + 1
            pooled = acc / jnp.minimum(pos, w).astype(F32)
        else:
            pooled = acc * (1.0 / w)
        pooled = pooled - ug
        mixed = jnp.dot(pooled.astype(MXU_DTYPE), wg_ref[g], preferred_element_type=F32)
        zg = z_ref[0, :, cs].astype(F32)
        y_ref[0, :, cs] = (mixed * sc_ref[:, cs] * _silu(zg)).astype(y_ref.dtype)
    e_sc[0:POOL_HIST, :] = e_sc[T:T + POOL_HIST, :]


def _pool_mixer(u, hist, w_grp, scale, *, T, from_start):
    bsz, s, w2 = u.shape
    w = w2 // 2
    nt = s // T
    hmap = (lambda b, t: (0, 0, 0)) if hist.shape[0] == 1 else (lambda b, t: (b, 0, 0))
    return pl.pallas_call(
        functools.partial(_pool_kernel, T=T, from_start=from_start),
        out_shape=jax.ShapeDtypeStruct((bsz, s, w), ACT_DTYPE), grid=(bsz, nt),
        in_specs=[pl.BlockSpec((1, T, w), lambda b, t: (b, t, 0)),
                  pl.BlockSpec((1, T, w), lambda b, t: (b, t, 1)),
                  pl.BlockSpec((1, POOL_HIST, w), hmap),
                  pl.BlockSpec((len(POOL_WINDOWS), POOL_GC, POOL_GC), lambda b, t: (0, 0, 0)),
                  pl.BlockSpec((1, w), lambda b, t: (0, 0))],
        out_specs=pl.BlockSpec((1, T, w), lambda b, t: (b, t, 0)),
        scratch_shapes=[pltpu.VMEM((POOL_HIST + T, w), F32)],
        compiler_params=_cparams(("parallel", "arbitrary")),
        name="pool_mixer",
    )(u, u, hist, w_grp, scale.reshape(1, w).astype(F32))


def _pool_band_matrices(T):
    t = np.arange(T)[:, None]
    k = np.arange(T)[None, :]
    th = np.arange(POOL_HIST)[:, None]
    ph = np.arange(POOL_HIST)[None, :] - POOL_HIST
    main = [((k <= t) & (k > t - w)) / w - (k == t) for w in POOL_WINDOWS]
    hist = [(ph > th - w) / w for w in POOL_WINDOWS]
    return np.stack(main).astype(np.float32), np.stack(hist).astype(np.float32)


def _pool_long_kernel(u_ref, z_ref, h0_ref, wg_ref, sc_ref, bm_ref, bh_ref, y_ref, hist_sc, *, T):
    t = pl.program_id(1)

    @pl.when(t == 0)
    def _():
        hist_sc[...] = h0_ref[0]

    for g in range(len(POOL_WINDOWS)):
        cs = slice(g * POOL_GC, (g + 1) * POOL_GC)
        pooled = jnp.dot(bm_ref[g], u_ref[0, :, cs].astype(MXU_DTYPE), preferred_element_type=F32)
        head = pooled[0:POOL_HIST] + jnp.dot(bh_ref[g], hist_sc[:, cs].astype(MXU_DTYPE),
                                             preferred_element_type=F32)
        pooled = jnp.concatenate([head, pooled[POOL_HIST:]], axis=0)
        mixed = jnp.dot(pooled.astype(MXU_DTYPE), wg_ref[g], preferred_element_type=F32)
        zg = z_ref[0, :, cs].astype(F32)
        y_ref[0, :, cs] = (mixed * sc_ref[:, cs] * _silu(zg)).astype(y_ref.dtype)
    hist_sc[...] = u_ref[0, T - POOL_HIST:T, 0:hist_sc.shape[1]]


def _pool_mixer_long(u, hist, w_grp, scale, *, T):
    assert jnp.dtype(ACT_DTYPE).itemsize <= jnp.dtype(MXU_DTYPE).itemsize
    bsz, s, w2 = u.shape
    w = w2 // 2
    nt = s // T
    ng = len(POOL_WINDOWS)
    bm, bh = _pool_band_matrices(T)
    hmap = (lambda b, t: (0, 0, 0)) if hist.shape[0] == 1 else (lambda b, t: (b, 0, 0))
    return pl.pallas_call(
        functools.partial(_pool_long_kernel, T=T),
        out_shape=jax.ShapeDtypeStruct((bsz, s, w), ACT_DTYPE), grid=(bsz, nt),
        in_specs=[pl.BlockSpec((1, T, w), lambda b, t: (b, t, 0)),
                  pl.BlockSpec((1, T, w), lambda b, t: (b, t, 1)),
                  pl.BlockSpec((1, POOL_HIST, w), hmap),
                  pl.BlockSpec((ng, POOL_GC, POOL_GC), lambda b, t: (0, 0, 0)),
                  pl.BlockSpec((1, w), lambda b, t: (0, 0)),
                  pl.BlockSpec((ng, T, T), lambda b, t: (0, 0, 0)),
                  pl.BlockSpec((ng, POOL_HIST, POOL_HIST), lambda b, t: (0, 0, 0))],
        out_specs=pl.BlockSpec((1, T, w), lambda b, t: (b, t, 0)),
        scratch_shapes=[pltpu.VMEM((POOL_HIST, w), ACT_DTYPE)],
        compiler_params=_cparams(("parallel", "arbitrary")),
        name="pool_mixer_long",
    )(u, u, hist, w_grp, scale.reshape(1, w).astype(F32),
      jnp.asarray(bm, MXU_DTYPE), jnp.asarray(bh, MXU_DTYPE))


def _rope_tables(pos):
    half = ROT_DIM // 2
    inv = np.power(ROPE_THETA, -np.arange(half, dtype=np.float64) / half)
    ang = np.asarray(pos, np.float64)[:, None] * inv[None, :]
    cos, sin = np.cos(ang), np.sin(ang)
    pad = np.zeros((ang.shape[0], SW_HD - ROT_DIM))
    ta = np.concatenate([cos, cos, pad + 1.0], axis=1)
    ts = np.concatenate([-sin, sin, pad], axis=1)
    return tuple(jnp.asarray(np.concatenate([t, t], axis=1), F32) for t in (ta, ts))


def _rope_partner_matrix():
    half = ROT_DIM // 2
    p = np.zeros((LANE, LANE), np.float32)
    for d in range(LANE):
        if d % SW_HD < half:
            p[d + half, d] = 1.0
        elif d % SW_HD < ROT_DIM:
            p[d - half, d] = 1.0
    return p


def _pick_tile(m, pref):
    t = pref
    while m % t:
        t //= 2
    return t


def kernel(x_prompt, x_sample, state_mlstm_C, state_mlstm_n, state_mlstm_m, cache_swa_k, cache_swa_v, state_pool, meta_tokens, norm_g, final_norm_g, mlstm_w_in, mlstm_b_if, mlstm_norm_g, mlstm_w_out, swa_w_in, swa_sinks, swa_w_out, pool_w_in, pool_w_grp, pool_scale, pool_w_out):
    bp, sp, d = x_prompt.shape
    bs, ss, _ = x_sample.shape
    depth = norm_g.shape[0]
    mp = bp * sp
    ms_rows = bs * ss
    small = ms_rows + N_META

    xp = x_prompt.reshape(mp, d)
    xs = jnp.concatenate([x_sample.reshape(ms_rows, d), meta_tokens.astype(x_prompt.dtype)], axis=0)
    tm_p = _pick_tile(sp, 1024)
    tm_o = _pick_tile(sp, 512)
    ml_chunk = _pick_tile(sp, 256)
    pool_tile = _pick_tile(sp, 256)

    mlstm_w_t = jnp.swapaxes(mlstm_w_in, 1, 2).astype(MXU_DTYPE)
    pC, pn, pm, pk, pv, pp = [], [], [], [], [], []
    sC, sn, sm, sk, sv, s_pool = [], [], [], [], [], []
    for i in range(depth):
        kind, j = i % N_MIXERS, i // N_MIXERS
        last = i == depth - 1
        fin = final_norm_g if last else None
        if kind == 0:
            gate_w = jnp.pad(mlstm_w_t[j, ML_MAIN_W:], ((0, ML_GATE_PAD - 2 * ML_HEADS), (0, 0)))
            up, gcp, gtp = _proj_in(xp, norm_g[i], mlstm_w_t, tm=tm_p, tn=2048, n=ML_MAIN_W, gate_w=gate_w,
                                    w_transposed=True, w_layer=j)
            us, gcs, gts = _proj_in(xs, norm_g[i], mlstm_w_t, tm=small, tn=1024, n=ML_MAIN_W, gate_w=gate_w,
                                    w_transposed=True, w_layer=j)
            zc = jnp.zeros((1, ML_HEADS, ML_DV, ML_DK), F32)
            zn = jnp.zeros((1, ML_HEADS, ML_DK), F32)
            zm = jnp.zeros((1, ML_HEADS, LANE), F32)
            ym, c_m, n_m, m_m = _mlstm_mixer(
                us[ms_rows:].reshape(1, N_META, -1), gcs[ms_rows:].reshape(1, N_META, -1),
                gts[:, ms_rows:].reshape(1, 1, 2 * ML_HEADS, N_META),
                mlstm_b_if[j], mlstm_norm_g[j], zc, zn, zm, L=N_META)
            ysm, c_s, n_s, m_s = _mlstm_mixer(
                us[:ms_rows].reshape(bs, ss, -1), gcs[:ms_rows].reshape(bs, ss, -1),
                gts[:, :ms_rows].reshape(2 * ML_HEADS, bs, 1, ss).transpose(1, 2, 0, 3),
                mlstm_b_if[j], mlstm_norm_g[j], state_mlstm_C[j].astype(F32), state_mlstm_n[j].astype(F32),
                jnp.broadcast_to(state_mlstm_m[j].astype(F32)[..., None], (bs, ML_HEADS, LANE)), L=ss)
            yp, c_p, n_p, m_p = _mlstm_mixer_long(
                up.reshape(bp, sp, -1), _gate_scan(gtp, mlstm_b_if[j], L=ml_chunk, tb=sp),
                mlstm_norm_g[j], c_m, n_m,
                jnp.broadcast_to(m_m[..., None], (1, ML_HEADS, LANE)), L=ml_chunk)
            pC.append(c_p); pn.append(n_p); pm.append(m_p)
            sC.append(c_s); sn.append(n_s); sm.append(m_s)
            w_out = mlstm_w_out[j]
        elif kind == 1:
            w_in = swa_w_in[j]
            wq, wk, wv, wz = jnp.split(w_in, [SW_W, SW_W + SW_KV_W, SW_W + 2 * SW_KV_W], axis=1)
            w_perm = jnp.concatenate([wq, wz, wk, wv], axis=1).astype(MXU_DTYPE)
            n_in = w_perm.shape[1]
            rope_mask = np.concatenate([np.ones(SW_W, bool), np.zeros(SW_W, bool),
                                        np.ones(SW_KV_W, bool), np.zeros(SW_KV_W, bool)])
            tabs_p = _rope_tables(N_META + np.arange(sp))
            pos_s = np.concatenate([np.tile(N_META + PAST_LEN + np.arange(ss), bs), np.arange(N_META)])
            tabs_s = _rope_tables(pos_s)
            tn_sw = n_in // 3
            up = _proj_in(xp, norm_g[i], w_perm, tm=tm_p, tn=tn_sw, rope_tabs=tabs_p, rope_mask=rope_mask)
            us = _proj_in(xs, norm_g[i], w_perm, tm=small, tn=tn_sw, rope_tabs=tabs_s, rope_mask=rope_mask)
            up = up.reshape(bp, sp, n_in)
            kv_s = us[:ms_rows, 2 * SW_W:].reshape(bs, ss, 2 * SW_KV_W)
            kv_m = us[ms_rows:, 2 * SW_W:].reshape(1, N_META, 2 * SW_KV_W)
            ym = _swa_mixer(us[ms_rows:].reshape(1, N_META, n_in), None, swa_sinks[j],
                            Lc=N_META, G=1, n_invalid=0)
            cache = jnp.concatenate([cache_swa_k[j].reshape(bs, WINDOW, SW_KV_W),
                                     cache_swa_v[j].reshape(bs, WINDOW, SW_KV_W)], axis=-1).astype(ACT_DTYPE)
            ysm = _swa_mixer(us[:ms_rows].reshape(bs, ss, n_in), cache, swa_sinks[j],
                             Lc=SW_CHUNK, G=ss // SW_CHUNK, n_invalid=0)
            hist = jnp.concatenate([jnp.zeros((1, WINDOW - N_META, 2 * SW_KV_W), ACT_DTYPE), kv_m], axis=1)
            yp = _swa_mixer(up, hist, swa_sinks[j], Lc=SW_CHUNK, G=_pick_tile(sp // SW_CHUNK, 8),
                            n_invalid=WINDOW - N_META)
            kv_p = up[:, -WINDOW:, 2 * SW_W:]
            pk.append(kv_p[:, :, :SW_KV_W].astype(F32).reshape(bp, WINDOW, SW_KV, SW_HD))
            pv.append(kv_p[:, :, SW_KV_W:].astype(F32).reshape(bp, WINDOW, SW_KV, SW_HD))
            k_new = kv_s[:, :, :SW_KV_W].astype(F32).reshape(bs, ss, SW_KV, SW_HD)
            v_new = kv_s[:, :, SW_KV_W:].astype(F32).reshape(bs, ss, SW_KV, SW_HD)
            sk.append(jnp.concatenate([cache_swa_k[j].astype(F32), k_new], axis=1)[:, -WINDOW:])
            sv.append(jnp.concatenate([cache_swa_v[j].astype(F32), v_new], axis=1)[:, -WINDOW:])
            w_out = swa_w_out[j]
        else:
            w_in = pool_w_in[j].astype(MXU_DTYPE)
            pw = w_in.shape[1] // 2
            w_grp = pool_w_grp[j].astype(MXU_DTYPE)
            up = _proj_in(xp, norm_g[i], w_in, tm=tm_p, tn=2048).reshape(bp, sp, 2 * pw)
            us = _proj_in(xs, norm_g[i], w_in, tm=small, tn=1024)
            u_m = us[ms_rows:].reshape(1, N_META, 2 * pw)
            u_s = us[:ms_rows].reshape(bs, ss, 2 * pw)
            ym = _pool_mixer(u_m, jnp.zeros((1, POOL_HIST, pw), F32), w_grp, pool_scale[j],
                             T=N_META, from_start=True)
            hist_s = jnp.pad(state_pool[j].astype(F32), ((0, 0), (POOL_HIST - state_pool.shape[2], 0), (0, 0)))
            ysm = _pool_mixer(u_s, hist_s, w_grp, pool_scale[j], T=ss, from_start=False)
            yp = _pool_mixer_long(up, u_m[:, :, :pw], w_grp, pool_scale[j], T=pool_tile)
            n_keep = state_pool.shape[2]
            pp.append(up[:, -n_keep:, :pw].astype(F32))
            s_pool.append(u_s[:, -n_keep:, :pw].astype(F32))
            w_out = pool_w_out[j]
        w_out = w_out.astype(MXU_DTYPE)
        y_small = jnp.concatenate([ysm.reshape(ms_rows, -1), ym.reshape(N_META, -1)], axis=0)
        xp = _proj_out(xp, yp.reshape(mp, -1), w_out, tm=tm_o, final_g=fin)
        xs = _proj_out(xs, y_small, w_out, tm=small, final_g=fin)

    y_prompt = xp.reshape(bp, sp, d)
    y_sample = xs[:ms_rows].reshape(bs, ss, d)
    return (y_prompt, y_sample,
            jnp.stack(pC), jnp.stack(pn), jnp.stack(pm), jnp.stack(pk), jnp.stack(pv), jnp.stack(pp),
            jnp.stack(sC), jnp.stack(sn), jnp.stack(sm), jnp.stack(sk), jnp.stack(sv), jnp.stack(s_pool))
```

```python
import functools

import numpy as np
import jax
import jax.numpy as jnp
from jax import lax
from jax.experimental import pallas as pl
from jax.experimental.pallas import tpu as pltpu

F32 = jnp.float32
MXU_DTYPE = jnp.bfloat16
ACT_DTYPE = jnp.bfloat16

EPS = 1e-6
N_META = 16
N_MIXERS = 3
PAST_LEN = 1024

ML_HEADS = 8
ML_DK = 128
ML_DV = 256
ML_QK_W = ML_HEADS * ML_DK
ML_V_W = ML_HEADS * ML_DV
ML_MAIN_W = 2 * ML_QK_W + 3 * ML_V_W
ML_GATE_PAD = 128

SW_HEADS = 32
SW_KV = 4
SW_HD = 64
SW_W = SW_HEADS * SW_HD
SW_KV_W = SW_KV * SW_HD
WINDOW = 128
SW_CHUNK = 64
ROT_DIM = 16
ROPE_THETA = 500000.0

POOL_WINDOWS = (2, 4, 8, 16)
POOL_GC = 512
POOL_HIST = 16

LANE = 128
VMEM_LIMIT_BYTES = 60 * 1024 * 1024


def _cparams(sem):
    return pltpu.CompilerParams(dimension_semantics=sem, vmem_limit_bytes=VMEM_LIMIT_BYTES)


def _nt_dot(a, b):
    return lax.dot_general(a, b, (((1,), (1,)), ((), ())), preferred_element_type=F32)


def _tn_dot(a, b):
    return lax.dot_general(a, b, (((0,), (0,)), ((), ())), preferred_element_type=F32)


def _sigmoid(x):
    return 0.5 + 0.5 * jnp.tanh(0.5 * x)


def _silu(x):
    return x * _sigmoid(x)


def _log_sigmoid(x):
    return jnp.minimum(x, 0.0) - jnp.log1p(jnp.exp(-jnp.abs(x)))


def _proj_in_kernel(*refs, rope, gates, w_transposed):
    x_ref, g_ref, w_ref = refs[:3]
    pos = 3
    if rope:
        ta_ref, ts_ref, perm_ref = refs[pos:pos + 3]
        pos += 3
    if gates:
        wg_ref = refs[pos]
        pos += 1
    o_ref = refs[pos]
    pos += 1
    if gates:
        gc_ref, gt_ref = refs[pos:pos + 2]
        pos += 2
    hn_sc = refs[pos]
    j = pl.program_id(1)

    @pl.when(j == 0)
    def _():
        x = x_ref[...]
        ms = jnp.mean(x * x, axis=1, keepdims=True)
        hn = (x * lax.rsqrt(ms + EPS) * g_ref[...]).astype(MXU_DTYPE)
        hn_sc[...] = hn
        if gates:
            gcol = _nt_dot(hn, wg_ref[...])
            gc_ref[...] = gcol[:, :2 * ML_HEADS]
            gt_ref[...] = gcol.T[:2 * ML_HEADS, :]

    if w_transposed:
        acc = _nt_dot(hn_sc[...], w_ref[...])
    else:
        acc = jnp.dot(hn_sc[...], w_ref[...], preferred_element_type=F32)
    if not rope:
        o_ref[...] = acc.astype(o_ref.dtype)
        return

    def store(pattern):
        pieces = []
        for l, rotate in enumerate(pattern):
            a = acc[:, l * LANE:(l + 1) * LANE]
            if rotate:
                partner = jnp.dot(a.astype(MXU_DTYPE), perm_ref[...], preferred_element_type=F32)
                a = a * ta_ref[...] + partner * ts_ref[...]
            pieces.append(a.astype(o_ref.dtype))
        o_ref[...] = jnp.concatenate(pieces, axis=1)

    for pattern in sorted(set(rope)):
        tiles = [t for t, p in enumerate(rope) if p == pattern]
        cond = functools.reduce(jnp.logical_or, [j == t for t in tiles])
        pl.when(cond)(functools.partial(store, pattern))


def _proj_in(x, g, w, *, tm, tn, n=None, rope_tabs=None, rope_mask=None, gate_w=None, w_transposed=False,
             w_layer=None):
    m, d = x.shape
    if n is None:
        n = w.shape[0] if w_transposed else w.shape[1]
    rope = ()
    if rope_tabs is not None:
        groups = rope_mask.reshape(n // LANE, LANE)
        assert (groups == groups[:, :1]).all()
        rope = tuple(tuple(bool(f) for f in groups[t * tn // LANE:(t + 1) * tn // LANE, 0])
                     for t in range(n // tn))
    gates = gate_w is not None
    grid = (m // tm, n // tn)
    in_specs = [pl.BlockSpec((tm, d), lambda i, j: (i, 0)),
                pl.BlockSpec((1, d), lambda i, j: (0, 0)),
                pl.BlockSpec((None, tn, d), lambda i, j: (w_layer, j, 0)) if w_layer is not None
                else pl.BlockSpec((tn, d), lambda i, j: (j, 0)) if w_transposed
                else pl.BlockSpec((d, tn), lambda i, j: (0, j))]
    args = [x, g.reshape(1, d), w]
    if rope:
        nrep = rope_tabs[0].shape[0] // tm
        for t in rope_tabs:
            in_specs.append(pl.BlockSpec((tm, LANE), lambda i, j, nrep=nrep: (i % nrep, 0)))
            args.append(t)
        in_specs.append(pl.BlockSpec((LANE, LANE), lambda i, j: (0, 0)))
        args.append(jnp.asarray(_rope_partner_matrix(), MXU_DTYPE))
    if gates:
        in_specs.append(pl.BlockSpec((ML_GATE_PAD, d), lambda i, j: (0, 0)))
        args.append(gate_w)
    out_shape = [jax.ShapeDtypeStruct((m, n), ACT_DTYPE)]
    out_specs = [pl.BlockSpec((tm, tn), lambda i, j: (i, j))]
    if gates:
        out_shape += [jax.ShapeDtypeStruct((m, 2 * ML_HEADS), F32),
                      jax.ShapeDtypeStruct((2 * ML_HEADS, m), F32)]
        out_specs += [pl.BlockSpec((tm, 2 * ML_HEADS), lambda i, j: (i, 0)),
                      pl.BlockSpec((2 * ML_HEADS, tm), lambda i, j: (0, i))]
    res = pl.pallas_call(
        functools.partial(_proj_in_kernel, rope=rope, gates=gates, w_transposed=w_transposed),
        out_shape=out_shape, grid=grid, in_specs=in_specs, out_specs=out_specs,
        scratch_shapes=[pltpu.VMEM((tm, d), MXU_DTYPE)],
        compiler_params=_cparams(("parallel", "arbitrary")),
        name="proj_in",
    )(*args)
    return res if gates else res[0]


def _proj_out_kernel(*refs, final_norm):
    y_ref, w_ref, x_ref = refs[:3]
    if final_norm:
        g_ref, o_ref = refs[3:5]
    else:
        o_ref = refs[3]
    acc = x_ref[...] + jnp.dot(y_ref[...].astype(MXU_DTYPE), w_ref[...], preferred_element_type=F32)
    if final_norm:
        ms = jnp.mean(acc * acc, axis=1, keepdims=True)
        acc = acc * lax.rsqrt(ms + EPS) * g_ref[...]
    o_ref[...] = acc


def _proj_out(x, y, w, *, tm, final_g=None):
    m, d = x.shape
    k = y.shape[1]
    final_norm = final_g is not None
    in_specs = [pl.BlockSpec((tm, k), lambda i: (i, 0)),
                pl.BlockSpec((k, d), lambda i: (0, 0), pipeline_mode=pl.Buffered(1)),
                pl.BlockSpec((tm, d), lambda i: (i, 0))]
    args = [y, w, x]
    if final_norm:
        in_specs.append(pl.BlockSpec((1, d), lambda i: (0, 0)))
        args.append(final_g.reshape(1, d))
    return pl.pallas_call(
        functools.partial(_proj_out_kernel, final_norm=final_norm),
        out_shape=jax.ShapeDtypeStruct((m, d), F32), grid=(m // tm,),
        in_specs=in_specs, out_specs=pl.BlockSpec((tm, d), lambda i: (i, 0)),
        compiler_params=_cparams(("parallel",)),
        name="proj_out",
    )(*args)


def _mlstm_kernel(q_ref, k_ref, v_ref, o_ref, z_ref, gc_ref, gt_ref, bc_ref, br_ref, ng_ref,
                  c0_ref, n0_ref, m0_ref, y_ref, cs_ref, ns_ref, ms_ref, *, L):
    c = pl.program_id(1)

    @pl.when(c == 0)
    def _():
        cs_ref[...] = c0_ref[...]
        ns_ref[...] = n0_ref[...]
        ms_ref[...] = m0_ref[...]

    row = lax.broadcasted_iota(jnp.int32, (L, L), 0)
    col = lax.broadcasted_iota(jnp.int32, (L, L), 1)
    tril = col <= row
    scale = ML_DK ** -0.5

    gc = gc_ref[0] + bc_ref[...]
    lane16 = lax.broadcasted_iota(jnp.int32, gc.shape, 1)
    gc = jnp.where(lane16 < ML_HEADS, gc, _log_sigmoid(gc))
    gt = gt_ref[0, 0] + br_ref[...]
    sub16 = lax.broadcasted_iota(jnp.int32, gt.shape, 0)
    gt = jnp.where(sub16 < ML_HEADS, gt, _log_sigmoid(gt))

    for h in range(ML_HEADS):
        li_col = gc[:, h:h + 1]
        lf_col = gc[:, ML_HEADS + h:ML_HEADS + h + 1]
        li_row = gt[h:h + 1, :]
        lf_row = gt[ML_HEADS + h:ML_HEADS + h + 1, :]
        q = q_ref[0, :, h * ML_DK:(h + 1) * ML_DK]
        k = k_ref[0, :, h * ML_DK:(h + 1) * ML_DK]
        v = v_ref[0, :, h * ML_DV:(h + 1) * ML_DV]
        c_old = cs_ref[0, h]
        n_old = ns_ref[0, h:h + 1, :]
        m_old = ms_ref[0, h:h + 1, 0:1]

        b_col = jnp.sum(jnp.where(tril, lf_row, 0.0), axis=1, keepdims=True)
        b_row = jnp.sum(jnp.where(row <= col, lf_col, 0.0), axis=0, keepdims=True)
        dmat = jnp.where(tril, b_col - b_row + li_row, -jnp.inf)
        inter = b_col + m_old
        m_t = jnp.maximum(inter, jnp.max(dmat, axis=1, keepdims=True))
        p = jnp.exp(dmat - m_t)
        w_inter = jnp.exp(inter - m_t)
        s = p * (_nt_dot(q, k) * scale)
        qf = q.astype(F32)
        kf = k.astype(F32)
        num = w_inter * _nt_dot(q, c_old.astype(MXU_DTYPE)) + jnp.dot(
            s.astype(MXU_DTYPE), v, preferred_element_type=F32)
        den = w_inter * jnp.sum(qf * n_old, axis=1, keepdims=True) + jnp.sum(s, axis=1, keepdims=True)
        hh = num / jnp.maximum(jnp.abs(den), jnp.exp(-m_t))

        hn = hh * lax.rsqrt(jnp.mean(hh * hh, axis=1, keepdims=True) + EPS)
        hn = hn * ng_ref[:, h * ML_DV:(h + 1) * ML_DV]
        og = o_ref[0, :, h * ML_DV:(h + 1) * ML_DV].astype(F32)
        zg = z_ref[0, :, h * ML_DV:(h + 1) * ML_DV].astype(F32)
        y_ref[0, :, h * ML_DV:(h + 1) * ML_DV] = (hn * _sigmoid(og) * _silu(zg)).astype(y_ref.dtype)

        b_end = jnp.sum(lf_row, axis=1, keepdims=True)
        a_end = b_end - b_col + li_col
        m_new = jnp.maximum(b_end + m_old, jnp.max(a_end, axis=0, keepdims=True))
        w_old = jnp.exp(b_end + m_old - m_new)
        w_s = jnp.exp(a_end - m_new)
        vw = (v.astype(F32) * w_s).astype(MXU_DTYPE)
        cs_ref[0, h] = w_old * c_old + _tn_dot(vw, k) * scale
        ns_ref[0, h:h + 1, :] = w_old * n_old + jnp.sum(kf * w_s, axis=0, keepdims=True) * scale
        ms_ref[0, h:h + 1, :] = jnp.broadcast_to(m_new, (1, LANE))


def _mlstm_mixer(u, gc, gt, b_if, norm_g, c0, n0, m0, *, L):
    bsz, s, _ = u.shape
    nc = s // L
    shared = c0.shape[0] == 1
    st = (lambda b, c: (0, 0, 0, 0)) if shared else (lambda b, c: (b, 0, 0, 0))
    st3 = (lambda b, c: (0, 0, 0)) if shared else (lambda b, c: (b, 0, 0))
    qk_b = ML_QK_W
    v_b = ML_V_W
    in_specs = [
        pl.BlockSpec((1, L, qk_b), lambda b, c: (b, c, 0)),
        pl.BlockSpec((1, L, qk_b), lambda b, c: (b, c, 1)),
        pl.BlockSpec((1, L, v_b), lambda b, c: (b, c, 1)),
        pl.BlockSpec((1, L, v_b), lambda b, c: (b, c, 2)),
        pl.BlockSpec((1, L, v_b), lambda b, c: (b, c, 3)),
        pl.BlockSpec((1, L, 2 * ML_HEADS), lambda b, c: (b, c, 0)),
        pl.BlockSpec((1, 1, 2 * ML_HEADS, L), lambda b, c: (b, c, 0, 0)),
        pl.BlockSpec((1, 2 * ML_HEADS), lambda b, c: (0, 0)),
        pl.BlockSpec((2 * ML_HEADS, 1), lambda b, c: (0, 0)),
        pl.BlockSpec((1, ML_V_W), lambda b, c: (0, 0)),
        pl.BlockSpec((1, ML_HEADS, ML_DV, ML_DK), st),
        pl.BlockSpec((1, ML_HEADS, ML_DK), st3),
        pl.BlockSpec((1, ML_HEADS, LANE), st3),
    ]
    out_shape = [jax.ShapeDtypeStruct((bsz, s, ML_V_W), ACT_DTYPE),
                 jax.ShapeDtypeStruct((bsz, ML_HEADS, ML_DV, ML_DK), F32),
                 jax.ShapeDtypeStruct((bsz, ML_HEADS, ML_DK), F32),
                 jax.ShapeDtypeStruct((bsz, ML_HEADS, LANE), F32)]
    out_specs = [pl.BlockSpec((1, L, ML_V_W), lambda b, c: (b, c, 0)),
                 pl.BlockSpec((1, ML_HEADS, ML_DV, ML_DK), lambda b, c: (b, 0, 0, 0)),
                 pl.BlockSpec((1, ML_HEADS, ML_DK), lambda b, c: (b, 0, 0)),
                 pl.BlockSpec((1, ML_HEADS, LANE), lambda b, c: (b, 0, 0))]
    b_flat = b_if.reshape(2 * ML_HEADS).astype(F32)
    y, cN, nN, mN = pl.pallas_call(
        functools.partial(_mlstm_kernel, L=L),
        out_shape=out_shape, grid=(bsz, nc), in_specs=in_specs, out_specs=out_specs,
        compiler_params=_cparams(("parallel", "arbitrary")),
        name="mlstm_mixer",
    )(u, u, u, u, u, gc, gt, b_flat.reshape(1, -1), b_flat.reshape(-1, 1),
      norm_g.reshape(1, ML_V_W).astype(F32), c0, n0, m0)
    return y, cN, nN, mN[:, :, 0]


ML_ST_W = ML_DV + LANE
ML_EXT_ROWS = 16


def _split3(x):
    x1 = x.astype(jnp.bfloat16).astype(F32)
    r1 = x - x1
    x2 = r1.astype(jnp.bfloat16).astype(F32)
    x3 = (r1 - x2).astype(jnp.bfloat16).astype(F32)
    return x1, x2, x3


def _gate_scan_kernel(g_ref, br_ref, o_ref, *, L):
    H = ML_HEADS
    g = g_ref[...] + br_ref[...]
    li = g[0:H]
    pos = lax.rem(lax.broadcasted_iota(jnp.int32, li.shape, 1), L)
    b = _log_sigmoid(g[H:2 * H])
    d = 1
    while d < L:
        b = b + jnp.where(pos >= d, pltpu.roll(b, d, 1), 0.0)
        d *= 2
    beta0 = li - b
    cm = beta0
    d = 1
    while d < L:
        cm = jnp.maximum(cm, jnp.where(pos >= d, pltpu.roll(cm, d, 1), -jnp.inf))
        d *= 2
    o_ref[0:H, :] = beta0
    o_ref[H:2 * H, :] = b
    o_ref[2 * H:3 * H, :] = cm


def _gate_scan(gt, b_if, *, L, tb):
    m = gt.shape[1]
    return pl.pallas_call(
        functools.partial(_gate_scan_kernel, L=L),
        out_shape=jax.ShapeDtypeStruct((3 * ML_HEADS, m), F32), grid=(m // tb,),
        in_specs=[pl.BlockSpec((2 * ML_HEADS, tb), lambda i: (0, i)),
                  pl.BlockSpec((2 * ML_HEADS, 1), lambda i: (0, 0))],
        out_specs=pl.BlockSpec((3 * ML_HEADS, tb), lambda i: (0, i)),
        compiler_params=_cparams(("parallel",)),
        name="mlstm_gate_scan",
    )(gt, b_if.reshape(2 * ML_HEADS, 1).astype(F32))


def _mlstm_long_kernel(q_ref, k_ref, v_ref, o_ref, z_ref, gs_ref, ng_ref, c0_ref, n0_ref, m0_ref,
                       y_ref, cs_ref, ns_ref, ms_ref, st_sc, *, L, CH):
    c = pl.program_id(1)
    nc = pl.num_programs(1)
    H = ML_HEADS
    bf = jnp.bfloat16

    @pl.when(c == 0)
    def _():
        for h in range(H):
            st_sc[h, :, 0:ML_DV] = c0_ref[0, h].T
            st_sc[h, :, ML_DV:ML_ST_W] = jnp.broadcast_to(n0_ref[0, h:h + 1, :], (ML_DK, LANE)).T
        ms_ref[...] = m0_ref[...]

    row = lax.broadcasted_iota(jnp.int32, (L, L), 0)
    col = lax.broadcasted_iota(jnp.int32, (L, L), 1)
    tril = col <= row

    ones_l = jnp.ones((3, L), F32)
    ones_s = jnp.ones((3, LANE), F32)
    sub = lax.broadcasted_iota(jnp.int32, (ML_EXT_ROWS, LANE), 0)
    rhs_f = jnp.where((sub < 3) | ((sub >= 6) & (sub < 9)), 1.0, 0.0)
    ones_v = jnp.ones((L, LANE), MXU_DTYPE)
    mean_w = jnp.full((ML_DV, LANE), 1.0 / ML_DV, MXU_DTYPE)

    def chunk_scalars(cc):
        ls = slice(cc * L, (cc + 1) * L)
        beta0 = gs_ref[0:H, ls]
        b = gs_ref[H:2 * H, ls]
        cm = gs_ref[2 * H:3 * H, ls]
        m_old = ms_ref[0]
        mm = jnp.maximum(jnp.concatenate([m_old] * (L // LANE), axis=1), cm)
        return (_split3(-mm * LOG2E), _split3((beta0 + jnp.log(jnp.float32(ML_DK ** -0.5))) * LOG2E),
                _split3(-b * LOG2E), _split3(m_old * LOG2E))

    def front(cc, h, scal):
        a3, b3, n3, o3 = scal
        hs = slice(h, h + 1)
        rs_ = slice(cc * L, (cc + 1) * L)
        lhs = jnp.concatenate([a3[0][hs], a3[1][hs], a3[2][hs], ones_l, n3[0][hs], n3[1][hs], n3[2][hs],
                               jnp.zeros((ML_EXT_ROWS - 9, L), F32)], axis=0)
        rhs_e = jnp.concatenate([ones_l, b3[0][hs], b3[1][hs], b3[2][hs],
                                 jnp.zeros((ML_EXT_ROWS - 6, L), F32)], axis=0)
        rhs_g = jnp.concatenate([ones_s, o3[0][hs], o3[1][hs], o3[2][hs],
                                 jnp.zeros((ML_EXT_ROWS - 6, LANE), F32)], axis=0)
        rhs = jnp.concatenate([rhs_g, rhs_f, rhs_e], axis=1).astype(bf)
        ext = _tn_dot(lhs.astype(bf), rhs)
        ef = ext[:, LANE:2 * LANE]
        wg = jnp.exp2(ext[:, 0:LANE])
        fl = jnp.exp2(ef)
        p = jnp.exp2(jnp.where(tril, ext[:, 2 * LANE:], -jnp.inf))

        q = q_ref[0, rs_, h * ML_DK:(h + 1) * ML_DK]
        k = k_ref[0, rs_, h * ML_DK:(h + 1) * ML_DK]
        v = v_ref[0, rs_, h * ML_DV:(h + 1) * ML_DV]
        s = p * _nt_dot(q, k)
        qw = q.astype(F32) * wg
        x = jnp.concatenate([qw.astype(MXU_DTYPE), s.astype(MXU_DTYPE)], axis=1)
        vo = jnp.concatenate([v.astype(MXU_DTYPE), ones_v], axis=1)
        st = st_sc[h]
        res = jnp.dot(x, jnp.concatenate([st.astype(MXU_DTYPE), vo], axis=0), preferred_element_type=F32)

        ktw = (k.astype(F32).T * p[L - 1:L, :]).astype(MXU_DTYPE)
        w_old = wg[L - 1:L, :]
        st_sc[h] = st * jnp.concatenate([w_old] * (ML_ST_W // LANE), axis=1) + jnp.dot(
            ktw, vo, preferred_element_type=F32)
        ms_ref[0, hs, :] = ef[L - 1:L, :] * (-1.0 / LOG2E)
        return res, fl

    def back(cc, h, res, fl):
        rs_ = slice(cc * L, (cc + 1) * L)
        r = 1.0 / jnp.maximum(jnp.abs(res[:, ML_DV:]), fl)
        hh = res[:, :ML_DV] * jnp.concatenate([r, r], axis=1)
        msq = jnp.dot((hh * hh).astype(MXU_DTYPE), mean_w, preferred_element_type=F32)
        rs = lax.rsqrt(msq + EPS)
        hn = hh * jnp.concatenate([rs, rs], axis=1) * ng_ref[:, h * ML_DV:(h + 1) * ML_DV]
        og = o_ref[0, rs_, h * ML_DV:(h + 1) * ML_DV].astype(F32)
        zg = z_ref[0, rs_, h * ML_DV:(h + 1) * ML_DV].astype(F32)
        gate = zg * (1.0 + jnp.tanh(0.5 * og)) * (1.0 + jnp.tanh(0.5 * zg))
        y_ref[0, rs_, h * ML_DV:(h + 1) * ML_DV] = (hn * gate).astype(y_ref.dtype)

    items = [(cc, h) for cc in range(CH) for h in range(H)]
    pending = None
    scal = None
    for n_item in range(len(items) + 1):
        nxt = None
        if n_item < len(items):
            cc, h = items[n_item]
            if h == 0:
                scal = chunk_scalars(cc)
            nxt = front(cc, h, scal)
        if pending is not None:
            back(*items[n_item - 1], *pending)
        pending = nxt

    @pl.when(c == nc - 1)
    def _():
        for h in range(H):
            cs_ref[0, h] = st_sc[h, :, 0:ML_DV].T
            ns_ref[0, h:h + 1, :] = st_sc[h, :, ML_DV:ML_ST_W].T[0:1, :]


def _mlstm_mixer_long(u, gscan, norm_g, c0, n0, m0, *, L):
    bsz, s, _ = u.shape
    CH = _pick_tile(s // L, 4)
    T = CH * L
    nc = s // T
    shared = c0.shape[0] == 1
    st = (lambda b, c: (0, 0, 0, 0)) if shared else (lambda b, c: (b, 0, 0, 0))
    st3 = (lambda b, c: (0, 0, 0)) if shared else (lambda b, c: (b, 0, 0))
    in_specs = [
        pl.BlockSpec((1, T, ML_QK_W), lambda b, c: (b, c, 0)),
        pl.BlockSpec((1, T, ML_QK_W), lambda b, c: (b, c, 1)),
        pl.BlockSpec((1, T, ML_V_W), lambda b, c: (b, c, 1)),
        pl.BlockSpec((1, T, ML_V_W), lambda b, c: (b, c, 2)),
        pl.BlockSpec((1, T, ML_V_W), lambda b, c: (b, c, 3)),
        pl.BlockSpec((3 * ML_HEADS, T), lambda b, c, nc=nc: (0, b * nc + c)),
        pl.BlockSpec((1, ML_V_W), lambda b, c: (0, 0)),
        pl.BlockSpec((1, ML_HEADS, ML_DV, ML_DK), st),
        pl.BlockSpec((1, ML_HEADS, ML_DK), st3),
        pl.BlockSpec((1, ML_HEADS, LANE), st3),
    ]
    out_shape = [jax.ShapeDtypeStruct((bsz, s, ML_V_W), ACT_DTYPE),
                 jax.ShapeDtypeStruct((bsz, ML_HEADS, ML_DV, ML_DK), F32),
                 jax.ShapeDtypeStruct((bsz, ML_HEADS, ML_DK), F32),
                 jax.ShapeDtypeStruct((bsz, ML_HEADS, LANE), F32)]
    out_specs = [pl.BlockSpec((1, T, ML_V_W), lambda b, c: (b, c, 0)),
                 pl.BlockSpec((1, ML_HEADS, ML_DV, ML_DK), lambda b, c: (b, 0, 0, 0)),
                 pl.BlockSpec((1, ML_HEADS, ML_DK), lambda b, c: (b, 0, 0)),
                 pl.BlockSpec((1, ML_HEADS, LANE), lambda b, c: (b, 0, 0))]
    y, cN, nN, mN = pl.pallas_call(
        functools.partial(_mlstm_long_kernel, L=L, CH=CH),
        out_shape=out_shape, grid=(bsz, nc), in_specs=in_specs, out_specs=out_specs,
        scratch_shapes=[pltpu.VMEM((ML_HEADS, ML_DK, ML_ST_W), F32)],
        compiler_params=_cparams(("parallel", "arbitrary")),
        name="mlstm_mixer_long",
    )(u, u, u, u, u, gscan, 0.25 * norm_g.reshape(1, ML_V_W).astype(F32), c0, n0, m0)
    return y, cN, nN, mN[:, :, 0]


LOG2E = 1.4426950408889634


def _swa_kernel(*refs, Lc, G, n_hist, n_invalid):
    sink_ref, q_ref, z_ref, kv_ref = refs[:4]
    pos = 4
    if n_hist:
        hist_ref = refs[pos]
        prev_refs = refs[pos + 1:pos + 1 + n_hist]
        pos += 1 + n_hist
    y_ref = refs[pos]
    c = pl.program_id(1)
    nk = (n_hist + 1) * Lc
    pairs = SW_HEADS // SW_KV // 2
    rows = pairs * Lc

    kv = kv_ref[0]
    if n_hist:
        prev = jnp.concatenate([r[0] for r in prev_refs], axis=0)
        kv = jnp.concatenate([jnp.where(c == 0, hist_ref[0], prev), kv], axis=0)
    kv = kv.astype(F32)
    nkr = kv.shape[0]
    lane = lax.broadcasted_iota(jnp.int32, (nkr, LANE), 1)
    low = lane < SW_HD
    blk = lax.broadcasted_iota(jnp.int32, (rows, 1), 0) // Lc
    ones_v = jnp.ones((nkr, LANE), MXU_DTYPE)

    def split(t, g):
        swapped = pltpu.roll(t, SW_HD, 1)
        if g % 2 == 0:
            lo, hi = t, swapped
        else:
            lo, hi = swapped, t
        return (jnp.where(low, lo, 0.0).astype(MXU_DTYPE), jnp.where(low, 0.0, hi).astype(MXU_DTYPE))

    for g in range(SW_KV):
        kt = kv[:, (g // 2) * LANE:(g // 2 + 1) * LANE] * (SW_HD ** -0.5 * LOG2E)
        vt = kv[:, SW_KV_W + (g // 2) * LANE:SW_KV_W + (g // 2 + 1) * LANE]
        k_par = split(kt, g)
        v_par = tuple(jnp.concatenate([vv, ones_v], axis=1) for vv in split(vt, g))
        sinks = []
        for par in range(2):
            sk = jnp.zeros((rows, 1), F32)
            for j in range(pairs):
                sk = jnp.where(blk == j, sink_ref[SW_HEADS // SW_KV * g + 2 * j + par] * LOG2E, sk)
            sinks.append(sk)

        def scores(i, k_par=k_par):
            ks = slice(i * Lc, i * Lc + nk)
            qs = jnp.concatenate([q_ref[0, i * Lc:(i + 1) * Lc, (pairs * g + j) * LANE:(pairs * g + j + 1) * LANE]
                                  for j in range(pairs)], axis=0)
            ss = []
            for par in range(2):
                s = _nt_dot(qs, k_par[par][ks])
                if n_invalid > i * Lc:
                    key_idx = (c * G + i) * Lc + lax.broadcasted_iota(jnp.int32, (1, nk), 1)
                    s = jnp.where(key_idx >= n_invalid, s, -jnp.inf)
                ss.append(s)
            return ss

        def attend(i, ss, g=g, v_par=v_par, sinks=sinks):
            ks = slice(i * Lc, i * Lc + nk)
            out = jnp.zeros((rows, LANE), F32)
            for par in range(2):
                mx = jnp.maximum(jnp.max(ss[par], axis=1, keepdims=True), sinks[par])
                p = jnp.exp2(ss[par] - mx)
                res = jnp.dot(p.astype(MXU_DTYPE), v_par[par][ks], preferred_element_type=F32)
                den = res[:, LANE:] + jnp.exp2(sinks[par] - mx)
                out = out + res[:, :LANE] / den
            for j in range(pairs):
                cs = slice((pairs * g + j) * LANE, (pairs * g + j + 1) * LANE)
                zg = z_ref[0, i * Lc:(i + 1) * Lc, cs].astype(F32)
                y_ref[0, i * Lc:(i + 1) * Lc, cs] = (out[j * Lc:(j + 1) * Lc] * _silu(zg)).astype(y_ref.dtype)

        pending = None
        for i in range(G + 1):
            nxt = scores(i) if i < G else None
            if pending is not None:
                attend(i - 1, pending)
            pending = nxt


def _swa_mixer(u, hist, sinks, *, Lc, G, n_invalid):
    bsz, s, _ = u.shape
    tq = G * Lc
    nc = s // tq
    kv_blk = 2 * SW_W // (2 * SW_KV_W)
    n_hist = 0 if hist is None else hist.shape[1] // Lc
    in_specs = [pl.BlockSpec(memory_space=pltpu.SMEM),
                pl.BlockSpec((1, tq, SW_W), lambda b, c: (b, c, 0)),
                pl.BlockSpec((1, tq, SW_W), lambda b, c: (b, c, 1)),
                pl.BlockSpec((1, tq, 2 * SW_KV_W), lambda b, c: (b, c, kv_blk))]
    args = [sinks.astype(F32), u, u, u]
    if n_hist:
        hmap = (lambda b, c: (0, 0, 0)) if hist.shape[0] == 1 else (lambda b, c: (b, 0, 0))
        in_specs.append(pl.BlockSpec((1, n_hist * Lc, 2 * SW_KV_W), hmap))
        args.append(hist)
        for i in range(n_hist):
            in_specs.append(pl.BlockSpec(
                (1, Lc, 2 * SW_KV_W),
                lambda b, c, i=i: (b, jnp.maximum(c * G - n_hist + i, 0), kv_blk)))
            args.append(u)
    return pl.pallas_call(
        functools.partial(_swa_kernel, Lc=Lc, G=G, n_hist=n_hist, n_invalid=n_invalid),
        out_shape=jax.ShapeDtypeStruct((bsz, s, SW_W), ACT_DTYPE), grid=(bsz, nc),
        in_specs=in_specs, out_specs=pl.BlockSpec((1, tq, SW_W), lambda b, c: (b, c, 0)),
        compiler_params=_cparams(("parallel", "parallel")),
        name="swa_mixer",
    )(*args)


def _pool_kernel(u_ref, z_ref, h0_ref, wg_ref, sc_ref, y_ref, e_sc, *, T, from_start):
    t = pl.program_id(1)

    @pl.when(t == 0)
    def _():
        e_sc[0:POOL_HIST, :] = h0_ref[0]

    u = u_ref[0].astype(F32)
    e_sc[POOL_HIST:POOL_HIST + T, :] = u
    for g, w in enumerate(POOL_WINDOWS):
        cs = slice(g * POOL_GC, (g + 1) * POOL_GC)
        ug = u[:, cs]
        acc = ug
        for d in range(1, w):
            acc = acc + e_sc[POOL_HIST - d:POOL_HIST - d + T, cs]
        if from_start:
            pos = t * T + lax.broadcasted_iota(jnp.int32, (T, 1), 0) + 1
            pooled = acc / jnp.minimum(pos, w).astype(F32)
        else:
            pooled = acc * (1.0 / w)
        pooled = pooled - ug
        mixed = jnp.dot(pooled.astype(MXU_DTYPE), wg_ref[g], preferred_element_type=F32)
        zg = z_ref[0, :, cs].astype(F32)
        y_ref[0, :, cs] = (mixed * sc_ref[:, cs] * _silu(zg)).astype(y_ref.dtype)
    e_sc[0:POOL_HIST, :] = e_sc[T:T + POOL_HIST, :]


def _pool_mixer(u, hist, w_grp, scale, *, T, from_start):
    bsz, s, w2 = u.shape
    w = w2 // 2
    nt = s // T
    hmap = (lambda b, t: (0, 0, 0)) if hist.shape[0] == 1 else (lambda b, t: (b, 0, 0))
    return pl.pallas_call(
        functools.partial(_pool_kernel, T=T, from_start=from_start),
        out_shape=jax.ShapeDtypeStruct((bsz, s, w), ACT_DTYPE), grid=(bsz, nt),
        in_specs=[pl.BlockSpec((1, T, w), lambda b, t: (b, t, 0)),
                  pl.BlockSpec((1, T, w), lambda b, t: (b, t, 1)),
                  pl.BlockSpec((1, POOL_HIST, w), hmap),
                  pl.BlockSpec((len(POOL_WINDOWS), POOL_GC, POOL_GC), lambda b, t: (0, 0, 0)),
                  pl.BlockSpec((1, w), lambda b, t: (0, 0))],
        out_specs=pl.BlockSpec((1, T, w), lambda b, t: (b, t, 0)),
        scratch_shapes=[pltpu.VMEM((POOL_HIST + T, w), F32)],
        compiler_params=_cparams(("parallel", "arbitrary")),
        name="pool_mixer",
    )(u, u, hist, w_grp, scale.reshape(1, w).astype(F32))


def _pool_band_matrices(T):
    t = np.arange(T)[:, None]
    k = np.arange(T)[None, :]
    th = np.arange(POOL_HIST)[:, None]
    ph = np.arange(POOL_HIST)[None, :] - POOL_HIST
    main = [((k <= t) & (k > t - w)) / w - (k == t) for w in POOL_WINDOWS]
    hist = [(ph > th - w) / w for w in POOL_WINDOWS]
    return np.stack(main).astype(np.float32), np.stack(hist).astype(np.float32)


def _pool_long_kernel(u_ref, z_ref, h0_ref, wg_ref, sc_ref, bm_ref, bh_ref, y_ref, hist_sc, *, T):
    t = pl.program_id(1)

    @pl.when(t == 0)
    def _():
        hist_sc[...] = h0_ref[0]

    for g in range(len(POOL_WINDOWS)):
        cs = slice(g * POOL_GC, (g + 1) * POOL_GC)
        pooled = jnp.dot(bm_ref[g], u_ref[0, :, cs].astype(MXU_DTYPE), preferred_element_type=F32)
        head = pooled[0:POOL_HIST] + jnp.dot(bh_ref[g], hist_sc[:, cs].astype(MXU_DTYPE),
                                             preferred_element_type=F32)
        pooled = jnp.concatenate([head, pooled[POOL_HIST:]], axis=0)
        mixed = jnp.dot(pooled.astype(MXU_DTYPE), wg_ref[g], preferred_element_type=F32)
        zg = z_ref[0, :, cs].astype(F32)
        y_ref[0, :, cs] = (mixed * sc_ref[:, cs] * _silu(zg)).astype(y_ref.dtype)
    hist_sc[...] = u_ref[0, T - POOL_HIST:T, 0:hist_sc.shape[1]]


def _pool_mixer_long(u, hist, w_grp, scale, *, T):
    assert jnp.dtype(ACT_DTYPE).itemsize <= jnp.dtype(MXU_DTYPE).itemsize
    bsz, s, w2 = u.shape
    w = w2 // 2
    nt = s // T
    ng = len(POOL_WINDOWS)
    bm, bh = _pool_band_matrices(T)
    hmap = (lambda b, t: (0, 0, 0)) if hist.shape[0] == 1 else (lambda b, t: (b, 0, 0))
    return pl.pallas_call(
        functools.partial(_pool_long_kernel, T=T),
        out_shape=jax.ShapeDtypeStruct((bsz, s, w), ACT_DTYPE), grid=(bsz, nt),
        in_specs=[pl.BlockSpec((1, T, w), lambda b, t: (b, t, 0)),
                  pl.BlockSpec((1, T, w), lambda b, t: (b, t, 1)),
                  pl.BlockSpec((1, POOL_HIST, w), hmap),
                  pl.BlockSpec((ng, POOL_GC, POOL_GC), lambda b, t: (0, 0, 0)),
                  pl.BlockSpec((1, w), lambda b, t: (0, 0)),
                  pl.BlockSpec((ng, T, T), lambda b, t: (0, 0, 0)),
                  pl.BlockSpec((ng, POOL_HIST, POOL_HIST), lambda b, t: (0, 0, 0))],
        out_specs=pl.BlockSpec((1, T, w), lambda b, t: (b, t, 0)),
        scratch_shapes=[pltpu.VMEM((POOL_HIST, w), ACT_DTYPE)],
        compiler_params=_cparams(("parallel", "arbitrary")),
        name="pool_mixer_long",
    )(u, u, hist, w_grp, scale.reshape(1, w).astype(F32),
      jnp.asarray(bm, MXU_DTYPE), jnp.asarray(bh, MXU_DTYPE))


def _rope_tables(pos):
    half = ROT_DIM // 2
    inv = np.power(ROPE_THETA, -np.arange(half, dtype=np.float64) / half)
    ang = np.asarray(pos, np.float64)[:, None] * inv[None, :]
    cos, sin = np.cos(ang), np.sin(ang)
    pad = np.zeros((ang.shape[0], SW_HD - ROT_DIM))
    ta = np.concatenate([cos, cos, pad + 1.0], axis=1)
    ts = np.concatenate([-sin, sin, pad], axis=1)
    return tuple(jnp.asarray(np.concatenate([t, t], axis=1), F32) for t in (ta, ts))


def _rope_partner_matrix():
    half = ROT_DIM // 2
    p = np.zeros((LANE, LANE), np.float32)
    for d in range(LANE):
        if d % SW_HD < half:
            p[d + half, d] = 1.0
        elif d % SW_HD < ROT_DIM:
            p[d - half, d] = 1.0
    return p


def _pick_tile(m, pref):
    t = pref
    while m % t:
        t //= 2
    return t


def kernel(x_prompt, x_sample, state_mlstm_C, state_mlstm_n, state_mlstm_m, cache_swa_k, cache_swa_v, state_pool, meta_tokens, norm_g, final_norm_g, mlstm_w_in, mlstm_b_if, mlstm_norm_g, mlstm_w_out, swa_w_in, swa_sinks, swa_w_out, pool_w_in, pool_w_grp, pool_scale, pool_w_out):
    bp, sp, d = x_prompt.shape
    bs, ss, _ = x_sample.shape
    depth = norm_g.shape[0]
    mp = bp * sp
    ms_rows = bs * ss
    small = ms_rows + N_META

    xp = x_prompt.reshape(mp, d)
    xs = jnp.concatenate([x_sample.reshape(ms_rows, d), meta_tokens.astype(x_prompt.dtype)], axis=0)
    tm_p = _pick_tile(sp, 1024)
    tm_o = _pick_tile(sp, 1024)
    ml_chunk = _pick_tile(sp, 256)
    pool_tile = _pick_tile(sp, 256)

    mlstm_w_t = jnp.swapaxes(mlstm_w_in, 1, 2).astype(MXU_DTYPE)
    pC, pn, pm, pk, pv, pp = [], [], [], [], [], []
    sC, sn, sm, sk, sv, s_pool = [], [], [], [], [], []
    for i in range(depth):
        kind, j = i % N_MIXERS, i // N_MIXERS
        last = i == depth - 1
        fin = final_norm_g if last else None
        if kind == 0:
            gate_w = jnp.pad(mlstm_w_t[j, ML_MAIN_W:], ((0, ML_GATE_PAD - 2 * ML_HEADS), (0, 0)))
            up, gcp, gtp = _proj_in(xp, norm_g[i], mlstm_w_t, tm=tm_p, tn=2048, n=ML_MAIN_W, gate_w=gate_w,
                                    w_transposed=True, w_layer=j)
            us, gcs, gts = _proj_in(xs, norm_g[i], mlstm_w_t, tm=small, tn=1024, n=ML_MAIN_W, gate_w=gate_w,
                                    w_transposed=True, w_layer=j)
            zc = jnp.zeros((1, ML_HEADS, ML_DV, ML_DK), F32)
            zn = jnp.zeros((1, ML_HEADS, ML_DK), F32)
            zm = jnp.zeros((1, ML_HEADS, LANE), F32)
            ym, c_m, n_m, m_m = _mlstm_mixer(
                us[ms_rows:].reshape(1, N_META, -1), gcs[ms_rows:].reshape(1, N_META, -1),
                gts[:, ms_rows:].reshape(1, 1, 2 * ML_HEADS, N_META),
                mlstm_b_if[j], mlstm_norm_g[j], zc, zn, zm, L=N_META)
            ysm, c_s, n_s, m_s = _mlstm_mixer(
                us[:ms_rows].reshape(bs, ss, -1), gcs[:ms_rows].reshape(bs, ss, -1),
                gts[:, :ms_rows].reshape(2 * ML_HEADS, bs, 1, ss).transpose(1, 2, 0, 3),
                mlstm_b_if[j], mlstm_norm_g[j], state_mlstm_C[j].astype(F32), state_mlstm_n[j].astype(F32),
                jnp.broadcast_to(state_mlstm_m[j].astype(F32)[..., None], (bs, ML_HEADS, LANE)), L=ss)
            yp, c_p, n_p, m_p = _mlstm_mixer_long(
                up.reshape(bp, sp, -1), _gate_scan(gtp, mlstm_b_if[j], L=ml_chunk, tb=sp),
                mlstm_norm_g[j], c_m, n_m,
                jnp.broadcast_to(m_m[..., None], (1, ML_HEADS, LANE)), L=ml_chunk)
            pC.append(c_p); pn.append(n_p); pm.append(m_p)
            sC.append(c_s); sn.append(n_s); sm.append(m_s)
            w_out = mlstm_w_out[j]
        elif kind == 1:
            w_in = swa_w_in[j]
            wq, wk, wv, wz = jnp.split(w_in, [SW_W, SW_W + SW_KV_W, SW_W + 2 * SW_KV_W], axis=1)
            w_perm = jnp.concatenate([wq, wz, wk, wv], axis=1).astype(MXU_DTYPE)
            n_in = w_perm.shape[1]
            rope_mask = np.concatenate([np.ones(SW_W, bool), np.zeros(SW_W, bool),
                                        np.ones(SW_KV_W, bool), np.zeros(SW_KV_W, bool)])
            tabs_p = _rope_tables(N_META + np.arange(sp))
            pos_s = np.concatenate([np.tile(N_META + PAST_LEN + np.arange(ss), bs), np.arange(N_META)])
            tabs_s = _rope_tables(pos_s)
            tn_sw = n_in // 3
            up = _proj_in(xp, norm_g[i], w_perm, tm=tm_p, tn=tn_sw, rope_tabs=tabs_p, rope_mask=rope_mask)
            us = _proj_in(xs, norm_g[i], w_perm, tm=small, tn=tn_sw, rope_tabs=tabs_s, rope_mask=rope_mask)
            up = up.reshape(bp, sp, n_in)
            kv_s = us[:ms_rows, 2 * SW_W:].reshape(bs, ss, 2 * SW_KV_W)
            kv_m = us[ms_rows:, 2 * SW_W:].reshape(1, N_META, 2 * SW_KV_W)
            ym = _swa_mixer(us[ms_rows:].reshape(1, N_META, n_in), None, swa_sinks[j],
                            Lc=N_META, G=1, n_invalid=0)
            cache = jnp.concatenate([cache_swa_k[j].reshape(bs, WINDOW, SW_KV_W),
                                     cache_swa_v[j].reshape(bs, WINDOW, SW_KV_W)], axis=-1).astype(ACT_DTYPE)
            ysm = _swa_mixer(us[:ms_rows].reshape(bs, ss, n_in), cache, swa_sinks[j],
                             Lc=SW_CHUNK, G=ss // SW_CHUNK, n_invalid=0)
            hist = jnp.concatenate([jnp.zeros((1, WINDOW - N_META, 2 * SW_KV_W), ACT_DTYPE), kv_m], axis=1)
            yp = _swa_mixer(up, hist, swa_sinks[j], Lc=SW_CHUNK, G=_pick_tile(sp // SW_CHUNK, 8),
                            n_invalid=WINDOW - N_META)
            kv_p = up[:, -WINDOW:, 2 * SW_W:]
            pk.append(kv_p[:, :, :SW_KV_W].astype(F32).reshape(bp, WINDOW, SW_KV, SW_HD))
            pv.append(kv_p[:, :, SW_KV_W:].astype(F32).reshape(bp, WINDOW, SW_KV, SW_HD))
            k_new = kv_s[:, :, :SW_KV_W].astype(F32).reshape(bs, ss, SW_KV, SW_HD)
            v_new = kv_s[:, :, SW_KV_W:].astype(F32).reshape(bs, ss, SW_KV, SW_HD)
            sk.append(jnp.concatenate([cache_swa_k[j].astype(F32), k_new], axis=1)[:, -WINDOW:])
            sv.append(jnp.concatenate([cache_swa_v[j].astype(F32), v_new], axis=1)[:, -WINDOW:])
            w_out = swa_w_out[j]
        else:
            w_in = pool_w_in[j].astype(MXU_DTYPE)
            pw = w_in.shape[1] // 2
            w_grp = pool_w_grp[j].astype(MXU_DTYPE)
            up = _proj_in(xp, norm_g[i], w_in, tm=tm_p, tn=2048).reshape(bp, sp, 2 * pw)
            us = _proj_in(xs, norm_g[i], w_in, tm=small, tn=1024)
            u_m = us[ms_rows:].reshape(1, N_META, 2 * pw)
            u_s = us[:ms_rows].reshape(bs, ss, 2 * pw)
            ym = _pool_mixer(u_m, jnp.zeros((1, POOL_HIST, pw), F32), w_grp, pool_scale[j],
                             T=N_META, from_start=True)
            hist_s = jnp.pad(state_pool[j].astype(F32), ((0, 0), (POOL_HIST - state_pool.shape[2], 0), (0, 0)))
            ysm = _pool_mixer(u_s, hist_s, w_grp, pool_scale[j], T=ss, from_start=False)
            yp = _pool_mixer_long(up, u_m[:, :, :pw], w_grp, pool_scale[j], T=pool_tile)
            n_keep = state_pool.shape[2]
            pp.append(up[:, -n_keep:, :pw].astype(F32))
            s_pool.append(u_s[:, -n_keep:, :pw].astype(F32))
            w_out = pool_w_out[j]
        w_out = w_out.astype(MXU_DTYPE)
        y_small = jnp.concatenate([ysm.reshape(ms_rows, -1), ym.reshape(N_META, -1)], axis=0)
        xp = _proj_out(xp, yp.reshape(mp, -1), w_out, tm=tm_o, final_g=fin)
        xs = _proj_out(xs, y_small, w_out, tm=small, final_g=fin)

    y_prompt = xp.reshape(bp, sp, d)
    y_sample = xs[:ms_rows].reshape(bs, ss, d)
    return (y_prompt, y_sample,
            jnp.stack(pC), jnp.stack(pn), jnp.stack(pm), jnp.stack(pk), jnp.stack(pv), jnp.stack(pp),
            jnp.stack(sC), jnp.stack(sn), jnp.stack(sm), jnp.stack(sk), jnp.stack(sv), jnp.stack(s_pool))
```
